```python
import math
import jax, jax.numpy as jnp
from jax import lax
import numpy as np

D_MODEL = 1024
BATCH = 1
SEQ = 16384
DEPTH = 1
DEC_BATCH = 16
DEC_SEQ = 4096
PAST_LEN = 128

N_HEADS = 8
QK_NOPE_DIM = 128
QK_ROPE_DIM = 64
QK_HEAD_DIM = QK_NOPE_DIM + QK_ROPE_DIM
V_HEAD_DIM = 128
Q_LORA_RANK = 384
KV_LORA_RANK = 256
ROPE_BASE = 10000.0
ATTN_SCALE = QK_HEAD_DIM ** -0.5
Q_BLOCK = 128
MLA_OUT = N_HEADS * V_HEAD_DIM
CHUNK = 128
SGU_GROUPS = 8
SGU_WIDTH = D_MODEL
SGU_GROUP_DIM = SGU_WIDTH // SGU_GROUPS
D_FF = 4 * D_MODEL
NORM_EPS = 1e-6
IN_COLS = Q_LORA_RANK + KV_LORA_RANK + QK_ROPE_DIM + 2 * SGU_WIDTH + 2 * D_MODEL
SPLITS = (
    Q_LORA_RANK,
    Q_LORA_RANK + KV_LORA_RANK,
    Q_LORA_RANK + KV_LORA_RANK + QK_ROPE_DIM,
    Q_LORA_RANK + KV_LORA_RANK + QK_ROPE_DIM + 2 * SGU_WIDTH,
)

kernel_name = "hybrid_mla_sgu_gated_encoder"


def _rmsnorm(x, g):
    xf = x.astype(jnp.float32)
    y = xf * lax.rsqrt(jnp.mean(xf * xf, axis=-1, keepdims=True) + NORM_EPS)
    return (y * g.astype(jnp.float32)).astype(x.dtype)


def _rope_tables(s):
    pos = jnp.arange(s, dtype=jnp.float32)
    inv = ROPE_BASE ** (-jnp.arange(0, QK_ROPE_DIM, 2, dtype=jnp.float32) / QK_ROPE_DIM)
    ang = pos[:, None] * inv[None, :]
    ang = jnp.concatenate([ang, ang], axis=-1)
    return jnp.cos(ang), jnp.sin(ang)


def _apply_rope(x, cos, sin):
    xf = x.astype(jnp.float32)
    half = QK_ROPE_DIM // 2
    rot = jnp.concatenate([-xf[..., half:], xf[..., :half]], axis=-1)
    return (xf * cos + rot * sin).astype(x.dtype)


def _dense_attention(q, k, v):
    b, s, h, dk = q.shape
    nb = s // Q_BLOCK
    qb = q.reshape(b, nb, Q_BLOCK, h, dk).transpose(1, 0, 2, 3, 4)

    def one_block(q_blk):
        sc = jnp.einsum('bqhd,bkhd->bhqk', q_blk, k).astype(jnp.float32) * ATTN_SCALE
        p = jax.nn.softmax(sc, axis=-1).astype(v.dtype)
        return jnp.einsum('bhqk,bkhd->bqhd', p, v)

    o = lax.map(one_block, qb)
    return o.transpose(1, 0, 2, 3, 4).reshape(b, s, h, V_HEAD_DIM)


def _block(x, norm_mix_g, w_in, q_norm_g, w_uq, kv_norm_g, w_ukv, sgu_norm_g, w_s, b_s,
           w_o, norm_ffn_g, w_ff1, w_ff2):
    b, s, _ = x.shape
    hn = _rmsnorm(x, norm_mix_g)
    z = hn @ w_in
    c_q, c_kv, k_rope, uv, gates = jnp.split(z, SPLITS, axis=-1)

    cos, sin = _rope_tables(s)
    q = (_rmsnorm(c_q, q_norm_g) @ w_uq).reshape(b, s, N_HEADS, QK_HEAD_DIM)
    q_nope = q[..., :QK_NOPE_DIM]
    q_rope = _apply_rope(q[..., QK_NOPE_DIM:], cos[:, None, :], sin[:, None, :])
    kv = (_rmsnorm(c_kv, kv_norm_g) @ w_ukv).reshape(b, s, N_HEADS, QK_NOPE_DIM + V_HEAD_DIM)
    k_nope = kv[..., :QK_NOPE_DIM]
    v = kv[..., QK_NOPE_DIM:]
    k_rope = _apply_rope(k_rope, cos, sin)
    q_full = jnp.concatenate([q_nope, q_rope], axis=-1)
    k_full = jnp.concatenate(
        [k_nope, jnp.broadcast_to(k_rope[:, :, None, :], (b, s, N_HEADS, QK_ROPE_DIM))], axis=-1)
    o_a = _dense_attention(q_full, k_full, v).reshape(b, s, MLA_OUT)

    uv = jax.nn.gelu(uv)
    u, vs = jnp.split(uv, 2, axis=-1)
    vs = _rmsnorm(vs, sgu_norm_g).reshape(b, s // CHUNK, CHUNK, SGU_GROUPS, SGU_GROUP_DIM)
    vs = jnp.einsum('gpq,bnqgc->bnpgc', w_s, vs) + b_s.T[None, None, :, :, None]
    o_b = u * vs.reshape(b, s, SGU_WIDTH)

    g_a, g_b = jnp.split(gates, 2, axis=-1)
    merged = jax.nn.sigmoid(g_a) * o_a + jax.nn.sigmoid(g_b) * o_b
    x = x + merged @ w_o

    hf = _rmsnorm(x, norm_ffn_g)
    x = x + jnp.square(jax.nn.relu(hf @ w_ff1)) @ w_ff2
    return x


def setup_inputs(seed: int = 0) -> dict:
    key = jax.random.key(seed)
    ks = jax.random.split(key, 16)
    f32 = jnp.float32

    def nrm(k, shape, scale):
        return jax.random.normal(k, shape, f32) * scale

    def gain(k, shape):
        return 1.0 + 0.01 * jax.random.normal(k, shape, f32)

    return {
        "x_prompt": jax.random.normal(ks[0], (BATCH, SEQ, D_MODEL), f32),
        "x_sample": jax.random.normal(ks[1], (DEC_BATCH, DEC_SEQ, D_MODEL), f32),
        "norm_mix_g": gain(ks[2], (DEPTH, D_MODEL)),
        "w_in": nrm(ks[3], (DEPTH, D_MODEL, IN_COLS), D_MODEL ** -0.5),
        "q_norm_g": gain(ks[4], (DEPTH, Q_LORA_RANK)),
        "w_uq": nrm(ks[5], (DEPTH, Q_LORA_RANK, N_HEADS * QK_HEAD_DIM), Q_LORA_RANK ** -0.5),
        "kv_norm_g": gain(ks[6], (DEPTH, KV_LORA_RANK)),
        "w_ukv": nrm(ks[7], (DEPTH, KV_LORA_RANK, N_HEADS * (QK_NOPE_DIM + V_HEAD_DIM)), KV_LORA_RANK ** -0.5),
        "sgu_norm_g": gain(ks[8], (DEPTH, SGU_WIDTH)),
        "w_s": nrm(ks[9], (DEPTH, SGU_GROUPS, CHUNK, CHUNK), CHUNK ** -0.5),
        "b_s": 1.0 + 0.02 * jax.random.normal(ks[10], (DEPTH, SGU_GROUPS, CHUNK), f32),
        "w_o": nrm(ks[11], (DEPTH, D_MODEL, D_MODEL), D_MODEL ** -0.5),
        "norm_ffn_g": gain(ks[12], (DEPTH, D_MODEL)),
        "w_ff1": nrm(ks[13], (DEPTH, D_MODEL, D_FF), D_MODEL ** -0.5),
        "w_ff2": nrm(ks[14], (DEPTH, D_FF, D_MODEL), D_FF ** -0.5),
        "final_norm_g": gain(ks[15], (D_MODEL,)),
    }


def reference(x_prompt, x_sample, norm_mix_g, w_in, q_norm_g, w_uq, kv_norm_g, w_ukv,
              sgu_norm_g, w_s, b_s, w_o, norm_ffn_g, w_ff1, w_ff2, final_norm_g):
    def run(x):
        for l in range(DEPTH):
            x = _block(x, norm_mix_g[l], w_in[l], q_norm_g[l], w_uq[l], kv_norm_g[l], w_ukv[l],
                       sgu_norm_g[l], w_s[l], b_s[l], w_o[l], norm_ffn_g[l], w_ff1[l], w_ff2[l])
        return _rmsnorm(x, final_norm_g)

    y_prompt = run(x_prompt)
    y_sample = run(x_sample)
    return (y_prompt, y_sample)
```

```python
import functools
import math

import jax
import jax.numpy as jnp
from jax import lax
from jax.experimental import pallas as pl
from jax.experimental.pallas import tpu as pltpu

D_MODEL = 1024
N_HEADS = 8
QK_NOPE_DIM = 128
QK_ROPE_DIM = 64
QK_HEAD_DIM = QK_NOPE_DIM + QK_ROPE_DIM
V_HEAD_DIM = 128
Q_LORA_RANK = 384
KV_LORA_RANK = 256
ROPE_BASE = 10000.0
ATTN_SCALE = QK_HEAD_DIM ** -0.5
CHUNK = 128
SGU_GROUPS = 8
SGU_WIDTH = D_MODEL
D_FF = 4 * D_MODEL
NORM_EPS = 1e-6

LANES = 128
QK_PAD_DIM = 2 * LANES
ROPE_HALF = QK_ROPE_DIM // 2

OFF_CQ = 0
OFF_CKV = OFF_CQ + Q_LORA_RANK
OFF_KR = OFF_CKV + KV_LORA_RANK
OFF_U = OFF_KR + LANES
OFF_VS = OFF_U + SGU_WIDTH
OFF_GA = OFF_VS + SGU_WIDTH
OFF_GB = OFF_GA + D_MODEL
IN_COLS_PAD = OFF_GB + D_MODEL

VMEM_LIMIT_BYTES = 56 * 1024 * 1024

BF16 = jnp.bfloat16
F32 = jnp.float32


def _rms(x, g):
    return x * lax.rsqrt(jnp.mean(x * x, axis=-1, keepdims=True) + NORM_EPS) * g


def _gelu_tanh(x):
    c = math.sqrt(2.0 / math.pi)
    return x * (0.5 * (1.0 + jnp.tanh(c * (x + 0.044715 * (x * x * x)))))


def _sigmoid(x):
    return 1.0 / (1.0 + jnp.exp(-x))


def _rope_block(xr, cos, sin_a, sin_b):
    return (xr * cos
            + pltpu.roll(xr, LANES - ROPE_HALF, 1) * sin_a
            + pltpu.roll(xr, ROPE_HALF, 1) * sin_b)


def _dot(a, b):
    return jnp.dot(a, b, preferred_element_type=F32)


def _pre_kernel(x_ref, gmix_ref, win_ref, gq_ref, wq_ref, gkv_ref, wk_ref, wv_ref,
                gsgu_ref, ws_ref, bias_ref, cos_ref, sina_ref, sinb_ref,
                q_ref, k_ref, v_ref, sga_ref, gob_ref):
    tm = x_ref.shape[0]
    hn = _rms(x_ref[...], gmix_ref[...]).astype(BF16)
    cos = cos_ref[...]
    sin_a = sina_ref[...]
    sin_b = sinb_ref[...]

    cq = _dot(hn, win_ref[:, OFF_CQ:OFF_CQ + Q_LORA_RANK])
    cqn = _rms(cq, gq_ref[...]).astype(BF16)
    qa = _dot(cqn, wq_ref[...])
    for h in range(N_HEADS):
        base = h * QK_PAD_DIM
        nope = qa[:, base:base + QK_NOPE_DIM] * ATTN_SCALE
        rope = _rope_block(qa[:, base + QK_NOPE_DIM:base + QK_PAD_DIM], cos, sin_a, sin_b)
        q_ref[h, :, 0:QK_NOPE_DIM] = nope.astype(BF16)
        q_ref[h, :, QK_NOPE_DIM:QK_PAD_DIM] = (rope * ATTN_SCALE).astype(BF16)

    ckv = _dot(hn, win_ref[:, OFF_CKV:OFF_CKV + KV_LORA_RANK])
    ckvn = _rms(ckv, gkv_ref[...]).astype(BF16)
    kn = _dot(ckvn, wk_ref[...])
    vv = _dot(ckvn, wv_ref[...])
    kr = _rope_block(_dot(hn, win_ref[:, OFF_KR:OFF_KR + LANES]), cos, sin_a, sin_b).astype(BF16)
    for h in range(N_HEADS):
        k_ref[h, :, 0:QK_NOPE_DIM] = kn[:, h * QK_NOPE_DIM:(h + 1) * QK_NOPE_DIM].astype(BF16)
        k_ref[h, :, QK_NOPE_DIM:QK_PAD_DIM] = kr
        v_ref[h] = vv[:, h * V_HEAD_DIM:(h + 1) * V_HEAD_DIM].astype(BF16)

    u = _gelu_tanh(_dot(hn, win_ref[:, OFF_U:OFF_U + SGU_WIDTH]))
    vs = _gelu_tanh(_dot(hn, win_ref[:, OFF_VS:OFF_VS + SGU_WIDTH]))
    vsn = _rms(vs, gsgu_ref[...]).astype(BF16)
    gb = _sigmoid(_dot(hn, win_ref[:, OFF_GB:OFF_GB + D_MODEL]))
    gd = SGU_WIDTH // SGU_GROUPS
    for c in range(tm // CHUNK):
        rows = slice(c * CHUNK, (c + 1) * CHUNK)
        for g in range(SGU_GROUPS):
            cols = slice(g * gd, (g + 1) * gd)
            mixed = _dot(ws_ref[g], vsn[rows, cols]) + bias_ref[:, cols]
            gob_ref[rows, cols] = (gb[rows, cols] * (u[rows, cols] * mixed)).astype(BF16)

    sga_ref[...] = _sigmoid(_dot(hn, win_ref[:, OFF_GA:OFF_GA + D_MODEL])).astype(BF16)


def _const_spec(shape):
    return pl.BlockSpec(shape, lambda i: (0,) * len(shape), pipeline_mode=pl.Buffered(1))


def _pre_call(x2d, seq, tm, gmix, win, gq, wq, gkv, wk, wv, gsgu, ws, bias, cos, sin_a, sin_b):
    t = x2d.shape[0]
    tiles_per_seq = seq // tm
    tab_spec = pl.BlockSpec((tm, LANES), lambda i: (i % tiles_per_seq, 0))
    return pl.pallas_call(
        _pre_kernel,
        grid=(t // tm,),
        in_specs=[
            pl.BlockSpec((tm, D_MODEL), lambda i: (i, 0)),
            _const_spec(gmix.shape), _const_spec(win.shape), _const_spec(gq.shape),
            _const_spec(wq.shape), _const_spec(gkv.shape), _const_spec(wk.shape),
            _const_spec(wv.shape), _const_spec(gsgu.shape), _const_spec(ws.shape),
            _const_spec(bias.shape), tab_spec, tab_spec, tab_spec,
        ],
        out_specs=[
            pl.BlockSpec((N_HEADS, tm, QK_PAD_DIM), lambda i: (0, i, 0)),
            pl.BlockSpec((N_HEADS, tm, QK_PAD_DIM), lambda i: (0, i, 0)),
            pl.BlockSpec((N_HEADS, tm, V_HEAD_DIM), lambda i: (0, i, 0)),
            pl.BlockSpec((tm, D_MODEL), lambda i: (i, 0)),
            pl.BlockSpec((tm, D_MODEL), lambda i: (i, 0)),
        ],
        out_shape=[
            jax.ShapeDtypeStruct((N_HEADS, t, QK_PAD_DIM), BF16),
            jax.ShapeDtypeStruct((N_HEADS, t, QK_PAD_DIM), BF16),
            jax.ShapeDtypeStruct((N_HEADS, t, V_HEAD_DIM), BF16),
            jax.ShapeDtypeStruct((t, D_MODEL), BF16),
            jax.ShapeDtypeStruct((t, D_MODEL), BF16),
        ],
        compiler_params=pltpu.CompilerParams(
            dimension_semantics=("arbitrary",), vmem_limit_bytes=VMEM_LIMIT_BYTES),
        name="pre_proj",
    )(x2d, gmix, win, gq, wq, gkv, wk, wv, gsgu, ws, bias, cos, sin_a, sin_b)


def _attn_kernel(q_ref, k_ref, v_ref, o_ref, *, tk):
    tq = q_ref.shape[1]
    seq = k_ref.shape[1]
    q = q_ref[0]

    def step(j, carry):
        m, l, acc = carry
        start = pl.multiple_of(j * tk, tk)
        kt = k_ref[0, pl.ds(start, tk), :]
        vt = v_ref[0, pl.ds(start, tk), :]
        s = lax.dot_general(q, kt, (((1,), (1,)), ((), ())), preferred_element_type=F32)
        m_new = jnp.maximum(m, jnp.max(s, axis=-1, keepdims=True))
        alpha = jnp.exp(m - m_new)
        p = jnp.exp(s - m_new)
        l_new = alpha * l + jnp.sum(p, axis=-1, keepdims=True)
        acc_new = alpha * acc + _dot(p.astype(BF16), vt)
        return m_new, l_new, acc_new

    init = (jnp.full((tq, 1), -jnp.inf, F32), jnp.zeros((tq, 1), F32),
            jnp.zeros((tq, V_HEAD_DIM), F32))
    _, l, acc = lax.fori_loop(0, seq // tk, step, init)
    o_ref[...] = (acc / l).astype(o_ref.dtype)


def _attn_call(q, k, v, batch, seq, tq, tk):
    t = batch * seq
    nq = seq // tq
    return pl.pallas_call(
        functools.partial(_attn_kernel, tk=tk),
        grid=(batch, N_HEADS, nq),
        in_specs=[
            pl.BlockSpec((1, tq, QK_PAD_DIM), lambda b, h, i: (h, b * nq + i, 0)),
            pl.BlockSpec((1, seq, QK_PAD_DIM), lambda b, h, i: (h, b, 0)),
            pl.BlockSpec((1, seq, V_HEAD_DIM), lambda b, h, i: (h, b, 0)),
        ],
        out_specs=pl.BlockSpec((tq, V_HEAD_DIM), lambda b, h, i: (b * nq + i, h)),
        out_shape=jax.ShapeDtypeStruct((t, N_HEADS * V_HEAD_DIM), BF16),
        compiler_params=pltpu.CompilerParams(
            dimension_semantics=("arbitrary", "arbitrary", "arbitrary"),
            vmem_limit_bytes=VMEM_LIMIT_BYTES),
        name="mla_attention",
    )(q, k, v)


def _post_kernel(x_ref, oa_ref, sga_ref, gob_ref, wo_ref, gffn_ref, w1_ref, w2_ref, gfin_ref,
                 y_ref):
    merged = sga_ref[...].astype(F32) * oa_ref[...].astype(F32) + gob_ref[...].astype(F32)
    x1 = x_ref[...] + _dot(merged.astype(BF16), wo_ref[...])
    hf = _rms(x1, gffn_ref[...]).astype(BF16)
    hid = jnp.square(jnp.maximum(_dot(hf, w1_ref[...]), 0.0)).astype(BF16)
    x2 = x1 + _dot(hid, w2_ref[...])
    y_ref[...] = _rms(x2, gfin_ref[...])


def _post_call(x2d, oa, sga, gob, wo, gffn, w1, w2, gfin, tm):
    t = x2d.shape[0]
    row_spec = pl.BlockSpec((tm, D_MODEL), lambda i: (i, 0))
    return pl.pallas_call(
        _post_kernel,
        grid=(t // tm,),
        in_specs=[row_spec, row_spec, row_spec, row_spec,
                  _const_spec(wo.shape), _const_spec(gffn.shape), _const_spec(w1.shape),
                  _const_spec(w2.shape), _const_spec(gfin.shape)],
        out_specs=row_spec,
        out_shape=jax.ShapeDtypeStruct((t, D_MODEL), F32),
        compiler_params=pltpu.CompilerParams(
            dimension_semantics=("arbitrary",), vmem_limit_bytes=VMEM_LIMIT_BYTES),
        name="post_ffn",
    )(x2d, oa, sga, gob, wo, gffn, w1, w2, gfin)


def _rope_tables(seq):
    pos = jnp.arange(seq, dtype=F32)
    inv = ROPE_BASE ** (-jnp.arange(0, QK_ROPE_DIM, 2, dtype=F32) / QK_ROPE_DIM)
    ang = pos[:, None] * inv[None, :]
    cos, sin = jnp.cos(ang), jnp.sin(ang)
    zero = jnp.zeros_like(cos)
    unused = jnp.zeros((seq, LANES - QK_ROPE_DIM), F32)
    cos_t = jnp.concatenate([cos, cos, unused], axis=-1)
    sin_a = jnp.concatenate([-sin, zero, unused], axis=-1)
    sin_b = jnp.concatenate([zero, sin, unused], axis=-1)
    return cos_t, sin_a, sin_b


def _prep_weights(norm_mix_g, w_in, q_norm_g, w_uq, kv_norm_g, w_ukv, sgu_norm_g, w_s, b_s,
                  w_o, norm_ffn_g, w_ff1, w_ff2, final_norm_g):
    kr_end = Q_LORA_RANK + KV_LORA_RANK + QK_ROPE_DIM
    win = jnp.concatenate(
        [w_in[:, :kr_end], jnp.zeros((D_MODEL, LANES - QK_ROPE_DIM), w_in.dtype), w_in[:, kr_end:]],
        axis=1).astype(BF16)
    wq = w_uq.reshape(Q_LORA_RANK, N_HEADS, QK_HEAD_DIM)
    wq = jnp.pad(wq, ((0, 0), (0, 0), (0, QK_PAD_DIM - QK_HEAD_DIM)))
    wq = wq.reshape(Q_LORA_RANK, N_HEADS * QK_PAD_DIM).astype(BF16)
    wkv = w_ukv.reshape(KV_LORA_RANK, N_HEADS, QK_NOPE_DIM + V_HEAD_DIM)
    wk = wkv[:, :, :QK_NOPE_DIM].reshape(KV_LORA_RANK, N_HEADS * QK_NOPE_DIM).astype(BF16)
    wv = wkv[:, :, QK_NOPE_DIM:].reshape(KV_LORA_RANK, N_HEADS * V_HEAD_DIM).astype(BF16)
    bias = jnp.repeat(b_s.T, SGU_WIDTH // SGU_GROUPS, axis=1).astype(F32)
    row = lambda g: g.reshape(1, -1).astype(F32)
    pre_w = (row(norm_mix_g), win, row(q_norm_g), wq, row(kv_norm_g), wk, wv, row(sgu_norm_g),
             w_s.astype(BF16), bias)
    post_w = (w_o.astype(BF16), row(norm_ffn_g), w_ff1.astype(BF16), w_ff2.astype(BF16),
              row(final_norm_g))
    return pre_w, post_w


def _run_stream(x, pre_w, post_w, tm_pre, tm_post, tq, tk):
    batch, seq, _ = x.shape
    x2d = x.reshape(batch * seq, D_MODEL)
    tables = _rope_tables(seq)
    q, k, v, sga, gob = _pre_call(x2d, seq, tm_pre, *pre_w, *tables)
    oa = _attn_call(q, k, v, batch, seq, tq, tk)
    y = _post_call(x2d, oa, sga, gob, *post_w, tm_post)
    return y.reshape(batch, seq, D_MODEL)


def kernel(x_prompt, x_sample, norm_mix_g, w_in, q_norm_g, w_uq, kv_norm_g, w_ukv, sgu_norm_g,
           w_s, b_s, w_o, norm_ffn_g, w_ff1, w_ff2, final_norm_g):
    pre_w, post_w = _prep_weights(
        norm_mix_g[0], w_in[0], q_norm_g[0], w_uq[0], kv_norm_g[0], w_ukv[0], sgu_norm_g[0],
        w_s[0], b_s[0], w_o[0], norm_ffn_g[0], w_ff1[0], w_ff2[0], final_norm_g)
    y_prompt = _run_stream(x_prompt, pre_w, post_w, tm_pre=512, tm_post=512, tq=512, tk=512)
    y_sample = _run_stream(x_sample, pre_w, post_w, tm_pre=512, tm_post=512, tq=512, tk=512)
    return (y_prompt, y_sample)
```

```python
import functools
import math

import jax
import jax.numpy as jnp
from jax import lax
from jax.experimental import pallas as pl
from jax.experimental.pallas import tpu as pltpu

D_MODEL = 1024
N_HEADS = 8
QK_NOPE_DIM = 128
QK_ROPE_DIM = 64
QK_HEAD_DIM = QK_NOPE_DIM + QK_ROPE_DIM
V_HEAD_DIM = 128
Q_LORA_RANK = 384
KV_LORA_RANK = 256
ROPE_BASE = 10000.0
ATTN_SCALE = QK_HEAD_DIM ** -0.5
CHUNK = 128
SGU_GROUPS = 8
SGU_WIDTH = D_MODEL
D_FF = 4 * D_MODEL
NORM_EPS = 1e-6

LANES = 128
QK_PAD_DIM = 2 * LANES
ROPE_HALF = QK_ROPE_DIM // 2
Q_SCALE = ATTN_SCALE * math.log2(math.e)

OFF_CQ = 0
OFF_CKV = OFF_CQ + Q_LORA_RANK
OFF_KR = OFF_CKV + KV_LORA_RANK
OFF_U = OFF_KR + LANES
OFF_VS = OFF_U + SGU_WIDTH
OFF_GA = OFF_VS + SGU_WIDTH
OFF_GB = OFF_GA + D_MODEL
IN_COLS_PAD = OFF_GB + D_MODEL

VMEM_LIMIT_BYTES = 56 * 1024 * 1024

BF16 = jnp.bfloat16
F32 = jnp.float32


def _rms(x, g):
    return x * lax.rsqrt(jnp.mean(x * x, axis=-1, keepdims=True) + NORM_EPS) * g


def _gelu_tanh(x):
    c = math.sqrt(2.0 / math.pi)
    return x * (0.5 * (1.0 + jnp.tanh(c * (x + 0.044715 * (x * x * x)))))


def _sigmoid(x):
    return 1.0 / (1.0 + jnp.exp(-x))


def _rope_block(xr, cos, sin_a, sin_b):
    return (xr * cos
            + pltpu.roll(xr, LANES - ROPE_HALF, 1) * sin_a
            + pltpu.roll(xr, ROPE_HALF, 1) * sin_b)


def _dot(a, b):
    return jnp.dot(a, b, preferred_element_type=F32)


def _pre_kernel(x_ref, gmix_ref, win_ref, gq_ref, wqt_ref, gkv_ref, wk_ref, wvt_ref,
                gsgu_ref, ws_ref, bias_ref, cos_ref, sina_ref, sinb_ref, cost_ref, sint_ref,
                qt_ref, k_ref, vt_ref, sga_ref, gob_ref):
    tm = x_ref.shape[0]
    hn = _rms(x_ref[...], gmix_ref[...]).astype(BF16)

    cq = _dot(hn, win_ref[:, OFF_CQ:OFF_CQ + Q_LORA_RANK])
    cqn_t = jnp.transpose(_rms(cq, gq_ref[...])).astype(BF16)
    qa_t = _dot(wqt_ref[...], cqn_t)
    cos_t = cost_ref[...]
    sin_t = sint_ref[...]
    for h in range(N_HEADS):
        base = h * QK_PAD_DIM
        r1 = base + QK_NOPE_DIM
        r2 = r1 + ROPE_HALF
        r3 = r2 + ROPE_HALF
        x1 = qa_t[r1:r2]
        x2 = qa_t[r2:r3]
        qt_ref[h, 0:QK_NOPE_DIM, :] = (qa_t[base:r1] * Q_SCALE).astype(BF16)
        qt_ref[h, QK_NOPE_DIM:QK_NOPE_DIM + ROPE_HALF, :] = (
            (x1 * cos_t - x2 * sin_t) * Q_SCALE).astype(BF16)
        qt_ref[h, QK_NOPE_DIM + ROPE_HALF:QK_HEAD_DIM, :] = (
            (x2 * cos_t + x1 * sin_t) * Q_SCALE).astype(BF16)
        qt_ref[h, QK_HEAD_DIM:QK_PAD_DIM, :] = jnp.zeros((QK_PAD_DIM - QK_HEAD_DIM, tm), BF16)

    ckv = _dot(hn, win_ref[:, OFF_CKV:OFF_CKV + KV_LORA_RANK])
    ckvn = _rms(ckv, gkv_ref[...])
    kn = _dot(ckvn.astype(BF16), wk_ref[...])
    vv_t = _dot(wvt_ref[...], jnp.transpose(ckvn).astype(BF16))
    kr = _rope_block(_dot(hn, win_ref[:, OFF_KR:OFF_KR + LANES]),
                     cos_ref[...], sina_ref[...], sinb_ref[...]).astype(BF16)
    for h in range(N_HEADS):
        k_ref[h, :, 0:QK_NOPE_DIM] = kn[:, h * QK_NOPE_DIM:(h + 1) * QK_NOPE_DIM].astype(BF16)
        k_ref[h, :, QK_NOPE_DIM:QK_PAD_DIM] = kr
        vt_ref[h] = vv_t[h * V_HEAD_DIM:(h + 1) * V_HEAD_DIM].astype(BF16)

    u = _gelu_tanh(_dot(hn, win_ref[:, OFF_U:OFF_U + SGU_WIDTH]))
    vs = _gelu_tanh(_dot(hn, win_ref[:, OFF_VS:OFF_VS + SGU_WIDTH]))
    vsn = _rms(vs, gsgu_ref[...]).astype(BF16)
    gb = _sigmoid(_dot(hn, win_ref[:, OFF_GB:OFF_GB + D_MODEL]))
    gd = SGU_WIDTH // SGU_GROUPS
    for c in range(tm // CHUNK):
        rows = slice(c * CHUNK, (c + 1) * CHUNK)
        for g in range(SGU_GROUPS):
            cols = slice(g * gd, (g + 1) * gd)
            mixed = _dot(ws_ref[g], vsn[rows, cols]) + bias_ref[:, cols]
            gob_ref[rows, cols] = (gb[rows, cols] * (u[rows, cols] * mixed)).astype(BF16)

    sga_ref[...] = _sigmoid(_dot(hn, win_ref[:, OFF_GA:OFF_GA + D_MODEL])).astype(BF16)


def _const_spec(shape):
    return pl.BlockSpec(shape, lambda i: (0,) * len(shape), pipeline_mode=pl.Buffered(1))


def _pre_call(x2d, seq, tm, gmix, win, gq, wqt, gkv, wk, wvt, gsgu, ws, bias,
              cos, sin_a, sin_b, cos_t, sin_t):
    t = x2d.shape[0]
    tiles_per_seq = seq // tm
    tab_spec = pl.BlockSpec((tm, LANES), lambda i: (i % tiles_per_seq, 0))
    tab_t_spec = pl.BlockSpec((ROPE_HALF, tm), lambda i: (0, i % tiles_per_seq))
    return pl.pallas_call(
        _pre_kernel,
        grid=(t // tm,),
        in_specs=[
            pl.BlockSpec((tm, D_MODEL), lambda i: (i, 0)),
            _const_spec(gmix.shape), _const_spec(win.shape), _const_spec(gq.shape),
            _const_spec(wqt.shape), _const_spec(gkv.shape), _const_spec(wk.shape),
            _const_spec(wvt.shape), _const_spec(gsgu.shape), _const_spec(ws.shape),
            _const_spec(bias.shape), tab_spec, tab_spec, tab_spec, tab_t_spec, tab_t_spec,
        ],
        out_specs=[
            pl.BlockSpec((N_HEADS, None, QK_PAD_DIM, tm), lambda i: (0, i, 0, 0)),
            pl.BlockSpec((N_HEADS, tm, QK_PAD_DIM), lambda i: (0, i, 0)),
            pl.BlockSpec((N_HEADS, None, V_HEAD_DIM, tm), lambda i: (0, i, 0, 0)),
            pl.BlockSpec((tm, D_MODEL), lambda i: (i, 0)),
            pl.BlockSpec((tm, D_MODEL), lambda i: (i, 0)),
        ],
        out_shape=[
            jax.ShapeDtypeStruct((N_HEADS, t // tm, QK_PAD_DIM, tm), BF16),
            jax.ShapeDtypeStruct((N_HEADS, t, QK_PAD_DIM), BF16),
            jax.ShapeDtypeStruct((N_HEADS, t // tm, V_HEAD_DIM, tm), BF16),
            jax.ShapeDtypeStruct((t, D_MODEL), BF16),
            jax.ShapeDtypeStruct((t, D_MODEL), BF16),
        ],
        compiler_params=pltpu.CompilerParams(
            dimension_semantics=("arbitrary",), vmem_limit_bytes=VMEM_LIMIT_BYTES),
        name="pre_proj",
    )(x2d, gmix, win, gq, wqt, gkv, wk, wvt, gsgu, ws, bias, cos, sin_a, sin_b, cos_t, sin_t)


def _attn_kernel(qt_ref, k_ref, vt_ref, o_ref, s_ref, mx_ref, p_ref, al_ref, m_ref, l_ref, acc_ref):
    n_q, _, tq = qt_ref.shape[1:]
    n_kv, _, tk = vt_ref.shape[1:]
    steps = n_q * n_kv
    sub = tk // 8

    def scores(t, slot):
        qi = t // n_kv
        kj = t % n_kv
        start = pl.multiple_of(kj * tk, tk)
        s_t = _dot(k_ref[0, pl.ds(start, tk), :], qt_ref[0, qi])
        s_ref[slot] = s_t
        mx_ref[slot] = jnp.max(s_t.reshape(sub, 8, tq), axis=0)

    def softmax(t, slot):
        qi = t // n_kv
        m_old = m_ref[qi]
        m_new = jnp.maximum(m_old, jnp.max(mx_ref[slot], axis=0, keepdims=True))
        alpha = jnp.exp2(m_old - m_new)
        p_t = jnp.exp2(s_ref[slot] - m_new)
        l_ref[qi] = alpha * l_ref[qi] + jnp.sum(p_t.reshape(sub, 8, tq), axis=0)
        p_ref[slot] = p_t.astype(BF16)
        al_ref[slot] = alpha
        m_ref[qi] = m_new

    def accumulate(t, slot):
        qi = t // n_kv
        kj = t % n_kv
        acc_ref[qi] = al_ref[slot] * acc_ref[qi] + _dot(vt_ref[0, kj], p_ref[slot])

    m_ref[...] = jnp.full(m_ref.shape, -jnp.inf, F32)
    l_ref[...] = jnp.zeros(l_ref.shape, F32)
    acc_ref[...] = jnp.zeros(acc_ref.shape, F32)
    scores(0, 0)
    scores(1, 1)
    softmax(0, 0)

    def body(i, carry):
        t0 = 2 * i
        scores(t0 + 2, 0)
        softmax(t0 + 1, 1)
        accumulate(t0, 0)
        scores(t0 + 3, 1)
        softmax(t0 + 2, 0)
        accumulate(t0 + 1, 1)
        return carry

    lax.fori_loop(0, steps // 2 - 1, body, 0)
    softmax(steps - 1, 1)
    accumulate(steps - 2, 0)
    accumulate(steps - 1, 1)

    for qi in range(n_q):
        l_tot = jnp.sum(l_ref[qi], axis=0, keepdims=True)
        o_ref[qi * tq:(qi + 1) * tq, :] = jnp.transpose(acc_ref[qi] / l_tot).astype(o_ref.dtype)


def _attn_call(qt, k, vt, batch, seq, n_q):
    _, _, _, tq = qt.shape
    _, _, _, tk = vt.shape
    t = batch * seq
    assert seq % (n_q * tq) == 0 and seq % tk == 0 and (n_q * (seq // tk)) % 2 == 0
    groups = seq // (n_q * tq)
    n_kv = seq // tk
    return pl.pallas_call(
        _attn_kernel,
        grid=(batch, N_HEADS, groups),
        in_specs=[
            pl.BlockSpec((1, n_q, QK_PAD_DIM, tq), lambda b, h, i: (h, b * groups + i, 0, 0)),
            pl.BlockSpec((1, seq, QK_PAD_DIM), lambda b, h, i: (h, b, 0)),
            pl.BlockSpec((1, n_kv, V_HEAD_DIM, tk), lambda b, h, i: (h, b, 0, 0)),
        ],
        out_specs=pl.BlockSpec((n_q * tq, V_HEAD_DIM), lambda b, h, i: (b * groups + i, h)),
        out_shape=jax.ShapeDtypeStruct((t, N_HEADS * V_HEAD_DIM), BF16),
        scratch_shapes=[
            pltpu.VMEM((2, tk, tq), F32),
            pltpu.VMEM((2, 8, tq), F32),
            pltpu.VMEM((2, tk, tq), BF16),
            pltpu.VMEM((2, 1, tq), F32),
            pltpu.VMEM((n_q, 1, tq), F32),
            pltpu.VMEM((n_q, 8, tq), F32),
            pltpu.VMEM((n_q, V_HEAD_DIM, tq), F32),
        ],
        compiler_params=pltpu.CompilerParams(
            dimension_semantics=("arbitrary", "arbitrary", "arbitrary"),
            vmem_limit_bytes=VMEM_LIMIT_BYTES),
        name="mla_attention",
    )(qt, k, vt)


def _post_kernel(x_ref, oa_ref, sga_ref, gob_ref, wo_ref, gffn_ref, w1_ref, w2_ref, gfin_ref,
                 y_ref):
    merged = sga_ref[...].astype(F32) * oa_ref[...].astype(F32) + gob_ref[...].astype(F32)
    x1 = x_ref[...] + _dot(merged.astype(BF16), wo_ref[...])
    hf = _rms(x1, gffn_ref[...]).astype(BF16)
    hid = jnp.square(jnp.maximum(_dot(hf, w1_ref[...]), 0.0)).astype(BF16)
    x2 = x1 + _dot(hid, w2_ref[...])
    y_ref[...] = _rms(x2, gfin_ref[...])


def _post_call(x2d, oa, sga, gob, wo, gffn, w1, w2, gfin, tm):
    t = x2d.shape[0]
    row_spec = pl.BlockSpec((tm, D_MODEL), lambda i: (i, 0))
    return pl.pallas_call(
        _post_kernel,
        grid=(t // tm,),
        in_specs=[row_spec, row_spec, row_spec, row_spec,
                  _const_spec(wo.shape), _const_spec(gffn.shape), _const_spec(w1.shape),
                  _const_spec(w2.shape), _const_spec(gfin.shape)],
        out_specs=row_spec,
        out_shape=jax.ShapeDtypeStruct((t, D_MODEL), F32),
        compiler_params=pltpu.CompilerParams(
            dimension_semantics=("arbitrary",), vmem_limit_bytes=VMEM_LIMIT_BYTES),
        name="post_ffn",
    )(x2d, oa, sga, gob, wo, gffn, w1, w2, gfin)


def _rope_tables(seq):
    pos = jnp.arange(seq, dtype=F32)
    inv = ROPE_BASE ** (-jnp.arange(0, QK_ROPE_DIM, 2, dtype=F32) / QK_ROPE_DIM)
    ang = pos[:, None] * inv[None, :]
    cos, sin = jnp.cos(ang), jnp.sin(ang)
    zero = jnp.zeros_like(cos)
    unused = jnp.zeros((seq, LANES - QK_ROPE_DIM), F32)
    cos_l = jnp.concatenate([cos, cos, unused], axis=-1)
    sin_a = jnp.concatenate([-sin, zero, unused], axis=-1)
    sin_b = jnp.concatenate([zero, sin, unused], axis=-1)
    return cos_l, sin_a, sin_b, cos.T, sin.T


def _prep_weights(norm_mix_g, w_in, q_norm_g, w_uq, kv_norm_g, w_ukv, sgu_norm_g, w_s, b_s,
                  w_o, norm_ffn_g, w_ff1, w_ff2, final_norm_g):
    kr_end = Q_LORA_RANK + KV_LORA_RANK + QK_ROPE_DIM
    win = jnp.concatenate(
        [w_in[:, :kr_end], jnp.zeros((D_MODEL, LANES - QK_ROPE_DIM), w_in.dtype), w_in[:, kr_end:]],
        axis=1).astype(BF16)
    wq = w_uq.reshape(Q_LORA_RANK, N_HEADS, QK_HEAD_DIM)
    wq = jnp.pad(wq, ((0, 0), (0, 0), (0, QK_PAD_DIM - QK_HEAD_DIM)))
    wqt = wq.reshape(Q_LORA_RANK, N_HEADS * QK_PAD_DIM).T.astype(BF16)
    wkv = w_ukv.reshape(KV_LORA_RANK, N_HEADS, QK_NOPE_DIM + V_HEAD_DIM)
    wk = wkv[:, :, :QK_NOPE_DIM].reshape(KV_LORA_RANK, N_HEADS * QK_NOPE_DIM).astype(BF16)
    wvt = wkv[:, :, QK_NOPE_DIM:].reshape(KV_LORA_RANK, N_HEADS * V_HEAD_DIM).T.astype(BF16)
    bias = jnp.repeat(b_s.T, SGU_WIDTH // SGU_GROUPS, axis=1).astype(F32)
    row = lambda g: g.reshape(1, -1).astype(F32)
    pre_w = (row(norm_mix_g), win, row(q_norm_g), wqt, row(kv_norm_g), wk, wvt, row(sgu_norm_g),
             w_s.astype(BF16), bias)
    post_w = (w_o.astype(BF16), row(norm_ffn_g), w_ff1.astype(BF16), w_ff2.astype(BF16),
              row(final_norm_g))
    return pre_w, post_w


TOKEN_TILE = 512
MAX_Q_TILES_PER_STEP = 8


def _run_stream(x, pre_w, post_w):
    batch, seq, _ = x.shape
    x2d = x.reshape(batch * seq, D_MODEL)
    tables = _rope_tables(seq)
    qt, k, vt, sga, gob = _pre_call(x2d, seq, TOKEN_TILE, *pre_w, *tables)
    n_q = min(MAX_Q_TILES_PER_STEP, seq // TOKEN_TILE)
    oa = _attn_call(qt, k, vt, batch, seq, n_q)
    y = _post_call(x2d, oa, sga, gob, *post_w, TOKEN_TILE)
    return y.reshape(batch, seq, D_MODEL)


def kernel(x_prompt, x_sample, norm_mix_g, w_in, q_norm_g, w_uq, kv_norm_g, w_ukv, sgu_norm_g,
           w_s, b_s, w_o, norm_ffn_g, w_ff1, w_ff2, final_norm_g):
    pre_w, post_w = _prep_weights(
        norm_mix_g[0], w_in[0], q_norm_g[0], w_uq[0], kv_norm_g[0], w_ukv[0], sgu_norm_g[0],
        w_s[0], b_s[0], w_o[0], norm_ffn_g[0], w_ff1[0], w_ff2[0], final_norm_g)
    return (_run_stream(x_prompt, pre_w, post_w), _run_stream(x_sample, pre_w, post_w))
```

```python
import functools
import math

import jax
import jax.numpy as jnp
from jax import lax
from jax.experimental import pallas as pl
from jax.experimental.pallas import tpu as pltpu

D_MODEL = 1024
N_HEADS = 8
QK_NOPE_DIM = 128
QK_ROPE_DIM = 64
QK_HEAD_DIM = QK_NOPE_DIM + QK_ROPE_DIM
V_HEAD_DIM = 128
V_ROWS = V_HEAD_DIM + 16
Q_LORA_RANK = 384
KV_LORA_RANK = 256
ROPE_BASE = 10000.0
ATTN_SCALE = QK_HEAD_DIM ** -0.5
CHUNK = 128
SGU_GROUPS = 8
SGU_WIDTH = D_MODEL
D_FF = 4 * D_MODEL
NORM_EPS = 1e-6

LANES = 128
QK_PAD_DIM = 2 * LANES
ROPE_HALF = QK_ROPE_DIM // 2
Q_SCALE = ATTN_SCALE * math.log2(math.e)

OFF_CQ = 0
OFF_CKV = OFF_CQ + Q_LORA_RANK
OFF_KR = OFF_CKV + KV_LORA_RANK
OFF_U = OFF_KR + LANES
OFF_VS = OFF_U + SGU_WIDTH
OFF_GA = OFF_VS + SGU_WIDTH
OFF_GB = OFF_GA + D_MODEL
IN_COLS_PAD = OFF_GB + D_MODEL

VMEM_LIMIT_BYTES = 56 * 1024 * 1024

BF16 = jnp.bfloat16
F32 = jnp.float32


def _rms(x, g):
    return x * lax.rsqrt(jnp.mean(x * x, axis=-1, keepdims=True) + NORM_EPS) * g


def _gelu_tanh(x):
    c = math.sqrt(2.0 / math.pi)
    return x * (0.5 * (1.0 + jnp.tanh(c * (x + 0.044715 * (x * x * x)))))


def _sigmoid(x):
    return 1.0 / (1.0 + jnp.exp(-x))


def _rope_block(xr, cos, sin_a, sin_b):
    return (xr * cos
            + pltpu.roll(xr, LANES - ROPE_HALF, 1) * sin_a
            + pltpu.roll(xr, ROPE_HALF, 1) * sin_b)


def _dot(a, b):
    return jnp.dot(a, b, preferred_element_type=F32)


def _pre_kernel(x_ref, gmix_ref, win_ref, gq_ref, wqt_ref, gkv_ref, wk_ref, wvt_ref,
                gsgu_ref, ws_ref, bias_ref, cos_ref, sina_ref, sinb_ref, cost_ref, sint_ref,
                qt_ref, k_ref, vt_ref, sga_ref, gob_ref):
    tm = x_ref.shape[0]
    hn = _rms(x_ref[...], gmix_ref[...]).astype(BF16)

    cq = _dot(hn, win_ref[:, OFF_CQ:OFF_CQ + Q_LORA_RANK])
    cqn_t = jnp.transpose(_rms(cq, gq_ref[...])).astype(BF16)
    qa_t = _dot(wqt_ref[...], cqn_t)
    cos_t = cost_ref[...]
    sin_t = sint_ref[...]
    for h in range(N_HEADS):
        base = h * QK_PAD_DIM
        r1 = base + QK_NOPE_DIM
        r2 = r1 + ROPE_HALF
        r3 = r2 + ROPE_HALF
        x1 = qa_t[r1:r2]
        x2 = qa_t[r2:r3]
        qt_ref[h, 0:QK_NOPE_DIM, :] = (qa_t[base:r1] * Q_SCALE).astype(BF16)
        qt_ref[h, QK_NOPE_DIM:QK_NOPE_DIM + ROPE_HALF, :] = (
            (x1 * cos_t - x2 * sin_t) * Q_SCALE).astype(BF16)
        qt_ref[h, QK_NOPE_DIM + ROPE_HALF:QK_HEAD_DIM, :] = (
            (x2 * cos_t + x1 * sin_t) * Q_SCALE).astype(BF16)
        qt_ref[h, QK_HEAD_DIM:QK_PAD_DIM, :] = jnp.zeros((QK_PAD_DIM - QK_HEAD_DIM, tm), BF16)

    ckv = _dot(hn, win_ref[:, OFF_CKV:OFF_CKV + KV_LORA_RANK])
    ckvn = _rms(ckv, gkv_ref[...])
    kn = _dot(ckvn.astype(BF16), wk_ref[...])
    vv_t = _dot(wvt_ref[...], jnp.transpose(ckvn).astype(BF16))
    kr = _rope_block(_dot(hn, win_ref[:, OFF_KR:OFF_KR + LANES]),
                     cos_ref[...], sina_ref[...], sinb_ref[...]).astype(BF16)
    for h in range(N_HEADS):
        k_ref[h, :, 0:QK_NOPE_DIM] = kn[:, h * QK_NOPE_DIM:(h + 1) * QK_NOPE_DIM].astype(BF16)
        k_ref[h, :, QK_NOPE_DIM:QK_PAD_DIM] = kr
        vt_ref[h, 0:V_HEAD_DIM, :] = vv_t[h * V_HEAD_DIM:(h + 1) * V_HEAD_DIM].astype(BF16)
        vt_ref[h, V_HEAD_DIM:V_ROWS, :] = jnp.ones((V_ROWS - V_HEAD_DIM, tm), BF16)

    u = _gelu_tanh(_dot(hn, win_ref[:, OFF_U:OFF_U + SGU_WIDTH]))
    vs = _gelu_tanh(_dot(hn, win_ref[:, OFF_VS:OFF_VS + SGU_WIDTH]))
    vsn = _rms(vs, gsgu_ref[...]).astype(BF16)
    gb = _sigmoid(_dot(hn, win_ref[:, OFF_GB:OFF_GB + D_MODEL]))
    gd = SGU_WIDTH // SGU_GROUPS
    for c in range(tm // CHUNK):
        rows = slice(c * CHUNK, (c + 1) * CHUNK)
        for g in range(SGU_GROUPS):
            cols = slice(g * gd, (g + 1) * gd)
            mixed = _dot(ws_ref[g], vsn[rows, cols]) + bias_ref[:, cols]
            gob_ref[rows, cols] = (gb[rows, cols] * (u[rows, cols] * mixed)).astype(BF16)

    sga_ref[...] = _sigmoid(_dot(hn, win_ref[:, OFF_GA:OFF_GA + D_MODEL])).astype(BF16)


def _const_spec(shape):
    return pl.BlockSpec(shape, lambda i: (0,) * len(shape), pipeline_mode=pl.Buffered(1))


def _pre_call(x2d, seq, tm, gmix, win, gq, wqt, gkv, wk, wvt, gsgu, ws, bias,
              cos, sin_a, sin_b, cos_t, sin_t):
    t = x2d.shape[0]
    tiles_per_seq = seq // tm
    tab_spec = pl.BlockSpec((tm, LANES), lambda i: (i % tiles_per_seq, 0))
    tab_t_spec = pl.BlockSpec((ROPE_HALF, tm), lambda i: (0, i % tiles_per_seq))
    return pl.pallas_call(
        _pre_kernel,
        grid=(t // tm,),
        in_specs=[
            pl.BlockSpec((tm, D_MODEL), lambda i: (i, 0)),
            _const_spec(gmix.shape), _const_spec(win.shape), _const_spec(gq.shape),
            _const_spec(wqt.shape), _const_spec(gkv.shape), _const_spec(wk.shape),
            _const_spec(wvt.shape), _const_spec(gsgu.shape), _const_spec(ws.shape),
            _const_spec(bias.shape), tab_spec, tab_spec, tab_spec, tab_t_spec, tab_t_spec,
        ],
        out_specs=[
            pl.BlockSpec((N_HEADS, None, QK_PAD_DIM, tm), lambda i: (0, i, 0, 0)),
            pl.BlockSpec((N_HEADS, tm, QK_PAD_DIM), lambda i: (0, i, 0)),
            pl.BlockSpec((N_HEADS, None, V_ROWS, tm), lambda i: (0, i, 0, 0)),
            pl.BlockSpec((tm, D_MODEL), lambda i: (i, 0)),
            pl.BlockSpec((tm, D_MODEL), lambda i: (i, 0)),
        ],
        out_shape=[
            jax.ShapeDtypeStruct((N_HEADS, t // tm, QK_PAD_DIM, tm), BF16),
            jax.ShapeDtypeStruct((N_HEADS, t, QK_PAD_DIM), BF16),
            jax.ShapeDtypeStruct((N_HEADS, t // tm, V_ROWS, tm), BF16),
            jax.ShapeDtypeStruct((t, D_MODEL), BF16),
            jax.ShapeDtypeStruct((t, D_MODEL), BF16),
        ],
        compiler_params=pltpu.CompilerParams(
            dimension_semantics=("arbitrary",), vmem_limit_bytes=VMEM_LIMIT_BYTES),
        name="pre_proj",
    )(x2d, gmix, win, gq, wqt, gkv, wk, wvt, gsgu, ws, bias, cos, sin_a, sin_b, cos_t, sin_t)


PIPE_SLOTS = 4
SOFTMAX_LEAD = 2


def _attn_kernel(qt_ref, k_ref, vt_ref, o_ref, *scratch):
    s_refs = scratch[0:PIPE_SLOTS]
    mx_refs = scratch[PIPE_SLOTS:2 * PIPE_SLOTS]
    p_refs = scratch[2 * PIPE_SLOTS:3 * PIPE_SLOTS]
    al_refs = scratch[3 * PIPE_SLOTS:4 * PIPE_SLOTS]
    m_ref, acc_ref = scratch[4 * PIPE_SLOTS:]
    n_q, _, tq = qt_ref.shape[1:]
    n_kv, _, tk = vt_ref.shape[1:]
    steps = n_q * n_kv
    sub = tk // 8

    def scores(t, slot):
        qi = t // n_kv
        kj = t % n_kv
        start = pl.multiple_of(kj * tk, tk)
        s_t = _dot(k_ref[0, pl.ds(start, tk), :], qt_ref[0, qi])
        s_refs[slot][...] = s_t
        mx_refs[slot][...] = jnp.max(s_t.reshape(sub, 8, tq), axis=0)

    def softmax(t, slot):
        qi = t // n_kv
        m_old = m_ref[qi]
        m_new = jnp.maximum(m_old, jnp.max(mx_refs[slot][...], axis=0, keepdims=True))
        p_refs[slot][...] = jnp.exp2((s_refs[slot][...] - m_new).astype(BF16))
        al_refs[slot][...] = jnp.exp2(m_old - m_new)
        m_ref[qi] = m_new

    def accumulate(t, slot):
        qi = t // n_kv
        kj = t % n_kv
        acc_ref[qi] = al_refs[slot][...] * acc_ref[qi] + _dot(vt_ref[0, kj], p_refs[slot][...])

    m_ref[...] = jnp.full(m_ref.shape, -jnp.inf, F32)
    acc_ref[...] = jnp.zeros(acc_ref.shape, F32)
    for t in range(PIPE_SLOTS):
        scores(t, t)
    for t in range(SOFTMAX_LEAD):
        softmax(t, t)

    def body(i, carry):
        t0 = PIPE_SLOTS * i
        for u in range(PIPE_SLOTS):
            accumulate(t0 + u, u)
            softmax(t0 + SOFTMAX_LEAD + u, (SOFTMAX_LEAD + u) % PIPE_SLOTS)
            scores(t0 + PIPE_SLOTS + u, u)
        return carry

    lax.fori_loop(0, steps // PIPE_SLOTS - 1, body, 0)
    for u in range(PIPE_SLOTS):
        t = steps - PIPE_SLOTS + u
        accumulate(t, u)
        if u + SOFTMAX_LEAD < PIPE_SLOTS:
            softmax(t + SOFTMAX_LEAD, (SOFTMAX_LEAD + u) % PIPE_SLOTS)

    for qi in range(n_q):
        out_t = acc_ref[qi, 0:V_HEAD_DIM, :] / acc_ref[qi, V_HEAD_DIM:V_HEAD_DIM + 1, :]
        o_ref[qi * tq:(qi + 1) * tq, :] = jnp.transpose(out_t).astype(o_ref.dtype)


def _attn_call(qt, k, vt, batch, seq, n_q):
    _, _, _, tq = qt.shape
    _, _, _, tk = vt.shape
    t = batch * seq
    assert seq % (n_q * tq) == 0 and seq % tk == 0 and (n_q * (seq // tk)) % PIPE_SLOTS == 0
    groups = seq // (n_q * tq)
    n_kv = seq // tk
    return pl.pallas_call(
        _attn_kernel,
        grid=(batch, N_HEADS, groups),
        in_specs=[
            pl.BlockSpec((1, n_q, QK_PAD_DIM, tq), lambda b, h, i: (h, b * groups + i, 0, 0)),
            pl.BlockSpec((1, seq, QK_PAD_DIM), lambda b, h, i: (h, b, 0)),
            pl.BlockSpec((1, n_kv, V_ROWS, tk), lambda b, h, i: (h, b, 0, 0)),
        ],
        out_specs=pl.BlockSpec((n_q * tq, V_HEAD_DIM), lambda b, h, i: (b * groups + i, h)),
        out_shape=jax.ShapeDtypeStruct((t, N_HEADS * V_HEAD_DIM), BF16),
        scratch_shapes=(
            [pltpu.VMEM((tk, tq), F32)] * PIPE_SLOTS
            + [pltpu.VMEM((8, tq), F32)] * PIPE_SLOTS
            + [pltpu.VMEM((tk, tq), BF16)] * PIPE_SLOTS
            + [pltpu.VMEM((1, tq), F32)] * PIPE_SLOTS
            + [pltpu.VMEM((n_q, 1, tq), F32),
               pltpu.VMEM((n_q, V_ROWS, tq), F32)]
        ),
        compiler_params=pltpu.CompilerParams(
            dimension_semantics=("arbitrary", "arbitrary", "arbitrary"),
            vmem_limit_bytes=VMEM_LIMIT_BYTES),
        name="mla_attention",
    )(qt, k, vt)


def _post_kernel(x_ref, oa_ref, sga_ref, gob_ref, wo_ref, gffn_ref, w1_ref, w2_ref, gfin_ref,
                 y_ref):
    merged = sga_ref[...].astype(F32) * oa_ref[...].astype(F32) + gob_ref[...].astype(F32)
    x1 = x_ref[...] + _dot(merged.astype(BF16), wo_ref[...])
    hf = _rms(x1, gffn_ref[...]).astype(BF16)
    hid = jnp.square(jnp.maximum(_dot(hf, w1_ref[...]), 0.0)).astype(BF16)
    x2 = x1 + _dot(hid, w2_ref[...])
    y_ref[...] = _rms(x2, gfin_ref[...])


def _post_call(x2d, oa, sga, gob, wo, gffn, w1, w2, gfin, tm):
    t = x2d.shape[0]
    row_spec = pl.BlockSpec((tm, D_MODEL), lambda i: (i, 0))
    return pl.pallas_call(
        _post_kernel,
        grid=(t // tm,),
        in_specs=[row_spec, row_spec, row_spec, row_spec,
                  _const_spec(wo.shape), _const_spec(gffn.shape), _const_spec(w1.shape),
                  _const_spec(w2.shape), _const_spec(gfin.shape)],
        out_specs=row_spec,
        out_shape=jax.ShapeDtypeStruct((t, D_MODEL), F32),
        compiler_params=pltpu.CompilerParams(
            dimension_semantics=("arbitrary",), vmem_limit_bytes=VMEM_LIMIT_BYTES),
        name="post_ffn",
    )(x2d, oa, sga, gob, wo, gffn, w1, w2, gfin)


def _rope_tables(seq):
    pos = jnp.arange(seq, dtype=F32)
    inv = ROPE_BASE ** (-jnp.arange(0, QK_ROPE_DIM, 2, dtype=F32) / QK_ROPE_DIM)
    ang = pos[:, None] * inv[None, :]
    cos, sin = jnp.cos(ang), jnp.sin(ang)
    zero = jnp.zeros_like(cos)
    unused = jnp.zeros((seq, LANES - QK_ROPE_DIM), F32)
    cos_l = jnp.concatenate([cos, cos, unused], axis=-1)
    sin_a = jnp.concatenate([-sin, zero, unused], axis=-1)
    sin_b = jnp.concatenate([zero, sin, unused], axis=-1)
    return cos_l, sin_a, sin_b, cos.T, sin.T


def _prep_weights(norm_mix_g, w_in, q_norm_g, w_uq, kv_norm_g, w_ukv, sgu_norm_g, w_s, b_s,
                  w_o, norm_ffn_g, w_ff1, w_ff2, final_norm_g):
    kr_end = Q_LORA_RANK + KV_LORA_RANK + QK_ROPE_DIM
    win = jnp.concatenate(
        [w_in[:, :kr_end], jnp.zeros((D_MODEL, LANES - QK_ROPE_DIM), w_in.dtype), w_in[:, kr_end:]],
        axis=1).astype(BF16)
    wq = w_uq.reshape(Q_LORA_RANK, N_HEADS, QK_HEAD_DIM)
    wq = jnp.pad(wq, ((0, 0), (0, 0), (0, QK_PAD_DIM - QK_HEAD_DIM)))
    wqt = wq.reshape(Q_LORA_RANK, N_HEADS * QK_PAD_DIM).T.astype(BF16)
    wkv = w_ukv.reshape(KV_LORA_RANK, N_HEADS, QK_NOPE_DIM + V_HEAD_DIM)
    wk = wkv[:, :, :QK_NOPE_DIM].reshape(KV_LORA_RANK, N_HEADS * QK_NOPE_DIM).astype(BF16)
    wvt = wkv[:, :, QK_NOPE_DIM:].reshape(KV_LORA_RANK, N_HEADS * V_HEAD_DIM).T.astype(BF16)
    bias = jnp.repeat(b_s.T, SGU_WIDTH // SGU_GROUPS, axis=1).astype(F32)
    row = lambda g: g.reshape(1, -1).astype(F32)
    pre_w = (row(norm_mix_g), win, row(q_norm_g), wqt, row(kv_norm_g), wk, wvt, row(sgu_norm_g),
             w_s.astype(BF16), bias)
    post_w = (w_o.astype(BF16), row(norm_ffn_g), w_ff1.astype(BF16), w_ff2.astype(BF16),
              row(final_norm_g))
    return pre_w, post_w


TOKEN_TILE = 512
MAX_Q_TILES_PER_STEP = 8


def _run_stream(x, pre_w, post_w):
    batch, seq, _ = x.shape
    x2d = x.reshape(batch * seq, D_MODEL)
    tables = _rope_tables(seq)
    qt, k, vt, sga, gob = _pre_call(x2d, seq, TOKEN_TILE, *pre_w, *tables)
    n_q = min(MAX_Q_TILES_PER_STEP, seq // TOKEN_TILE)
    oa = _attn_call(qt, k, vt, batch, seq, n_q)
    y = _post_call(x2d, oa, sga, gob, *post_w, TOKEN_TILE)
    return y.reshape(batch, seq, D_MODEL)


def kernel(x_prompt, x_sample, norm_mix_g, w_in, q_norm_g, w_uq, kv_norm_g, w_ukv, sgu_norm_g,
           w_s, b_s, w_o, norm_ffn_g, w_ff1, w_ff2, final_norm_g):
    pre_w, post_w = _prep_weights(
        norm_mix_g[0], w_in[0], q_norm_g[0], w_uq[0], kv_norm_g[0], w_ukv[0], sgu_norm_g[0],
        w_s[0], b_s[0], w_o[0], norm_ffn_g[0], w_ff1[0], w_ff2[0], final_norm_g)
    return (_run_stream(x_prompt, pre_w, post_w), _run_stream(x_sample, pre_w, post_w))
```

```python
import functools
import math

import jax
import jax.numpy as jnp
from jax import lax
from jax.experimental import pallas as pl
from jax.experimental.pallas import tpu as pltpu

D_MODEL = 1024
N_HEADS = 8
QK_NOPE_DIM = 128
QK_ROPE_DIM = 64
QK_HEAD_DIM = QK_NOPE_DIM + QK_ROPE_DIM
V_HEAD_DIM = 128
V_ROWS = V_HEAD_DIM + 16
Q_LORA_RANK = 384
KV_LORA_RANK = 256
ROPE_BASE = 10000.0
ATTN_SCALE = QK_HEAD_DIM ** -0.5
CHUNK = 128
SGU_GROUPS = 8
SGU_WIDTH = D_MODEL
D_FF = 4 * D_MODEL
NORM_EPS = 1e-6

LANES = 128
QK_PAD_DIM = 2 * LANES
ROPE_HALF = QK_ROPE_DIM // 2
Q_SCALE = ATTN_SCALE * math.log2(math.e)

OFF_CQ = 0
OFF_CKV = OFF_CQ + Q_LORA_RANK
OFF_KR = OFF_CKV + KV_LORA_RANK
OFF_U = OFF_KR + LANES
OFF_VS = OFF_U + SGU_WIDTH
OFF_GA = OFF_VS + SGU_WIDTH
OFF_GB = OFF_GA + D_MODEL
IN_COLS_PAD = OFF_GB + D_MODEL

VMEM_LIMIT_BYTES = 56 * 1024 * 1024

BF16 = jnp.bfloat16
F32 = jnp.float32


def _rms(x, g):
    return x * lax.rsqrt(jnp.mean(x * x, axis=-1, keepdims=True) + NORM_EPS) * g


def _gelu_tanh(x):
    c = math.sqrt(2.0 / math.pi)
    return x * (0.5 * (1.0 + jnp.tanh(c * (x + 0.044715 * (x * x * x)))))


def _sigmoid(x):
    return 1.0 / (1.0 + jnp.exp(-x))


def _rope_block(xr, cos, sin_a, sin_b):
    return (xr * cos
            + pltpu.roll(xr, LANES - ROPE_HALF, 1) * sin_a
            + pltpu.roll(xr, ROPE_HALF, 1) * sin_b)


def _dot(a, b):
    return jnp.dot(a, b, preferred_element_type=F32)


def _pre_kernel(x_ref, gmix_ref, win_ref, gq_ref, wqt_ref, gkv_ref, wk_ref, wvt_ref,
                gsgu_ref, ws_ref, bias_ref, cos_ref, sina_ref, sinb_ref, cost_ref, sint_ref,
                qt_ref, k_ref, vt_ref, sga_ref, gob_ref):
    tm = x_ref.shape[0]
    hn = _rms(x_ref[...], gmix_ref[...]).astype(BF16)

    cq = _dot(hn, win_ref[:, OFF_CQ:OFF_CQ + Q_LORA_RANK])
    cqn_t = jnp.transpose(_rms(cq, gq_ref[...])).astype(BF16)
    qa_t = _dot(wqt_ref[...], cqn_t)
    cos_t = cost_ref[...]
    sin_t = sint_ref[...]
    for h in range(N_HEADS):
        base = h * QK_PAD_DIM
        r1 = base + QK_NOPE_DIM
        r2 = r1 + ROPE_HALF
        r3 = r2 + ROPE_HALF
        x1 = qa_t[r1:r2]
        x2 = qa_t[r2:r3]
        qt_ref[h, 0:QK_NOPE_DIM, :] = (qa_t[base:r1] * Q_SCALE).astype(BF16)
        qt_ref[h, QK_NOPE_DIM:QK_NOPE_DIM + ROPE_HALF, :] = (
            (x1 * cos_t - x2 * sin_t) * Q_SCALE).astype(BF16)
        qt_ref[h, QK_NOPE_DIM + ROPE_HALF:QK_HEAD_DIM, :] = (
            (x2 * cos_t + x1 * sin_t) * Q_SCALE).astype(BF16)
        qt_ref[h, QK_HEAD_DIM:QK_PAD_DIM, :] = jnp.zeros((QK_PAD_DIM - QK_HEAD_DIM, tm), BF16)

    ckv = _dot(hn, win_ref[:, OFF_CKV:OFF_CKV + KV_LORA_RANK])
    ckvn = _rms(ckv, gkv_ref[...])
    kn = _dot(ckvn.astype(BF16), wk_ref[...])
    vv_t = _dot(wvt_ref[...], jnp.transpose(ckvn).astype(BF16))
    kr = _rope_block(_dot(hn, win_ref[:, OFF_KR:OFF_KR + LANES]),
                     cos_ref[...], sina_ref[...], sinb_ref[...]).astype(BF16)
    for h in range(N_HEADS):
        k_ref[h, :, 0:QK_NOPE_DIM] = kn[:, h * QK_NOPE_DIM:(h + 1) * QK_NOPE_DIM].astype(BF16)
        k_ref[h, :, QK_NOPE_DIM:QK_PAD_DIM] = kr
        vt_ref[h, 0:V_HEAD_DIM, :] = vv_t[h * V_HEAD_DIM:(h + 1) * V_HEAD_DIM].astype(BF16)
        vt_ref[h, V_HEAD_DIM:V_ROWS, :] = jnp.ones((V_ROWS - V_HEAD_DIM, tm), BF16)

    u = _gelu_tanh(_dot(hn, win_ref[:, OFF_U:OFF_U + SGU_WIDTH]))
    vs = _gelu_tanh(_dot(hn, win_ref[:, OFF_VS:OFF_VS + SGU_WIDTH]))
    vsn = _rms(vs, gsgu_ref[...]).astype(BF16)
    gb = _sigmoid(_dot(hn, win_ref[:, OFF_GB:OFF_GB + D_MODEL]))
    gd = SGU_WIDTH // SGU_GROUPS
    for c in range(tm // CHUNK):
        rows = slice(c * CHUNK, (c + 1) * CHUNK)
        for g in range(SGU_GROUPS):
            cols = slice(g * gd, (g + 1) * gd)
            mixed = _dot(ws_ref[g], vsn[rows, cols]) + bias_ref[:, cols]
            gob_ref[rows, cols] = (gb[rows, cols] * (u[rows, cols] * mixed)).astype(BF16)

    sga_ref[...] = _sigmoid(_dot(hn, win_ref[:, OFF_GA:OFF_GA + D_MODEL])).astype(BF16)


def _const_spec(shape):
    return pl.BlockSpec(shape, lambda i: (0,) * len(shape), pipeline_mode=pl.Buffered(1))


def _pre_call(x2d, seq, tm, gmix, win, gq, wqt, gkv, wk, wvt, gsgu, ws, bias,
              cos, sin_a, sin_b, cos_t, sin_t):
    t = x2d.shape[0]
    tiles_per_seq = seq // tm
    tab_spec = pl.BlockSpec((tm, LANES), lambda i: (i % tiles_per_seq, 0))
    tab_t_spec = pl.BlockSpec((ROPE_HALF, tm), lambda i: (0, i % tiles_per_seq))
    return pl.pallas_call(
        _pre_kernel,
        grid=(t // tm,),
        in_specs=[
            pl.BlockSpec((tm, D_MODEL), lambda i: (i, 0)),
            _const_spec(gmix.shape), _const_spec(win.shape), _const_spec(gq.shape),
            _const_spec(wqt.shape), _const_spec(gkv.shape), _const_spec(wk.shape),
            _const_spec(wvt.shape), _const_spec(gsgu.shape), _const_spec(ws.shape),
            _const_spec(bias.shape), tab_spec, tab_spec, tab_spec, tab_t_spec, tab_t_spec,
        ],
        out_specs=[
            pl.BlockSpec((N_HEADS, None, QK_PAD_DIM, tm), lambda i: (0, i, 0, 0)),
            pl.BlockSpec((N_HEADS, tm, QK_PAD_DIM), lambda i: (0, i, 0)),
            pl.BlockSpec((N_HEADS, None, V_ROWS, tm), lambda i: (0, i, 0, 0)),
            pl.BlockSpec((tm, D_MODEL), lambda i: (i, 0)),
            pl.BlockSpec((tm, D_MODEL), lambda i: (i, 0)),
        ],
        out_shape=[
            jax.ShapeDtypeStruct((N_HEADS, t // tm, QK_PAD_DIM, tm), BF16),
            jax.ShapeDtypeStruct((N_HEADS, t, QK_PAD_DIM), BF16),
            jax.ShapeDtypeStruct((N_HEADS, t // tm, V_ROWS, tm), BF16),
            jax.ShapeDtypeStruct((t, D_MODEL), BF16),
            jax.ShapeDtypeStruct((t, D_MODEL), BF16),
        ],
        compiler_params=pltpu.CompilerParams(
            dimension_semantics=("arbitrary",), vmem_limit_bytes=VMEM_LIMIT_BYTES),
        name="pre_proj",
    )(x2d, gmix, win, gq, wqt, gkv, wk, wvt, gsgu, ws, bias, cos, sin_a, sin_b, cos_t, sin_t)


PIPE_SLOTS = 4


def _attn_kernel(qt_ref, k_ref, vt_ref, o_ref, *scratch):
    s_refs = scratch[0:PIPE_SLOTS]
    mx_refs = scratch[PIPE_SLOTS:2 * PIPE_SLOTS]
    m_ref, acc_ref = scratch[2 * PIPE_SLOTS:]
    n_q, _, tq = qt_ref.shape[1:]
    n_kv, _, tk = vt_ref.shape[1:]
    steps = n_q * n_kv
    sub = tk // 8

    def scores(t, slot):
        qi = t // n_kv
        kj = t % n_kv
        start = pl.multiple_of(kj * tk, tk)
        s_t = _dot(k_ref[0, pl.ds(start, tk), :], qt_ref[0, qi])
        s_refs[slot][...] = s_t
        mx_refs[slot][...] = jnp.max(s_t.reshape(sub, 8, tq), axis=0)

    def update(t, slot):
        qi = t // n_kv
        kj = t % n_kv
        m_old = m_ref[qi]
        m_new = jnp.maximum(m_old, jnp.max(mx_refs[slot][...], axis=0, keepdims=True))
        p_t = jnp.exp2((s_refs[slot][...] - m_new).astype(BF16))
        acc_ref[qi] = jnp.exp2(m_old - m_new) * acc_ref[qi] + _dot(vt_ref[0, kj], p_t)
        m_ref[qi] = m_new

    m_ref[...] = jnp.full(m_ref.shape, -jnp.inf, F32)
    acc_ref[...] = jnp.zeros(acc_ref.shape, F32)
    for t in range(PIPE_SLOTS):
        scores(t, t)

    def body(i, carry):
        t0 = PIPE_SLOTS * i
        for u in range(PIPE_SLOTS):
            update(t0 + u, u)
            scores(t0 + PIPE_SLOTS + u, u)
        return carry

    lax.fori_loop(0, steps // PIPE_SLOTS - 1, body, 0)
    for u in range(PIPE_SLOTS):
        update(steps - PIPE_SLOTS + u, u)

    for qi in range(n_q):
        out_t = acc_ref[qi, 0:V_HEAD_DIM, :] / acc_ref[qi, V_HEAD_DIM:V_HEAD_DIM + 1, :]
        o_ref[qi * tq:(qi + 1) * tq, :] = jnp.transpose(out_t).astype(o_ref.dtype)


def _attn_call(qt, k, vt, batch, seq, n_q):
    _, _, _, tq = qt.shape
    _, _, _, tk = vt.shape
    t = batch * seq
    assert seq % (n_q * tq) == 0 and seq % tk == 0 and (n_q * (seq // tk)) % PIPE_SLOTS == 0
    groups = seq // (n_q * tq)
    n_kv = seq // tk
    return pl.pallas_call(
        _attn_kernel,
        grid=(batch, N_HEADS, groups),
        in_specs=[
            pl.BlockSpec((1, n_q, QK_PAD_DIM, tq), lambda b, h, i: (h, b * groups + i, 0, 0)),
            pl.BlockSpec((1, seq, QK_PAD_DIM), lambda b, h, i: (h, b, 0)),
            pl.BlockSpec((1, n_kv, V_ROWS, tk), lambda b, h, i: (h, b, 0, 0)),
        ],
        out_specs=pl.BlockSpec((n_q * tq, V_HEAD_DIM), lambda b, h, i: (b * groups + i, h)),
        out_shape=jax.ShapeDtypeStruct((t, N_HEADS * V_HEAD_DIM), BF16),
        scratch_shapes=(
            [pltpu.VMEM((tk, tq), F32)] * PIPE_SLOTS
            + [pltpu.VMEM((8, tq), F32)] * PIPE_SLOTS
            + [pltpu.VMEM((n_q, 1, tq), F32),
               pltpu.VMEM((n_q, V_ROWS, tq), F32)]
        ),
        compiler_params=pltpu.CompilerParams(
            dimension_semantics=("arbitrary", "arbitrary", "arbitrary"),
            vmem_limit_bytes=VMEM_LIMIT_BYTES),
        name="mla_attention",
    )(qt, k, vt)


def _post_kernel(x_ref, oa_ref, sga_ref, gob_ref, wo_ref, gffn_ref, w1_ref, w2_ref, gfin_ref,
                 y_ref):
    merged = sga_ref[...].astype(F32) * oa_ref[...].astype(F32) + gob_ref[...].astype(F32)
    x1 = x_ref[...] + _dot(merged.astype(BF16), wo_ref[...])
    hf = _rms(x1, gffn_ref[...]).astype(BF16)
    hid = jnp.square(jnp.maximum(_dot(hf, w1_ref[...]), 0.0)).astype(BF16)
    x2 = x1 + _dot(hid, w2_ref[...])
    y_ref[...] = _rms(x2, gfin_ref[...])


def _post_call(x2d, oa, sga, gob, wo, gffn, w1, w2, gfin, tm):
    t = x2d.shape[0]
    row_spec = pl.BlockSpec((tm, D_MODEL), lambda i: (i, 0))
    return pl.pallas_call(
        _post_kernel,
        grid=(t // tm,),
        in_specs=[row_spec, row_spec, row_spec, row_spec,
                  _const_spec(wo.shape), _const_spec(gffn.shape), _const_spec(w1.shape),
                  _const_spec(w2.shape), _const_spec(gfin.shape)],
        out_specs=row_spec,
        out_shape=jax.ShapeDtypeStruct((t, D_MODEL), F32),
        compiler_params=pltpu.CompilerParams(
            dimension_semantics=("arbitrary",), vmem_limit_bytes=VMEM_LIMIT_BYTES),
        name="post_ffn",
    )(x2d, oa, sga, gob, wo, gffn, w1, w2, gfin)


def _rope_tables(seq):
    pos = jnp.arange(seq, dtype=F32)
    inv = ROPE_BASE ** (-jnp.arange(0, QK_ROPE_DIM, 2, dtype=F32) / QK_ROPE_DIM)
    ang = pos[:, None] * inv[None, :]
    cos, sin = jnp.cos(ang), jnp.sin(ang)
    zero = jnp.zeros_like(cos)
    unused = jnp.zeros((seq, LANES - QK_ROPE_DIM), F32)
    cos_l = jnp.concatenate([cos, cos, unused], axis=-1)
    sin_a = jnp.concatenate([-sin, zero, unused], axis=-1)
    sin_b = jnp.concatenate([zero, sin, unused], axis=-1)
    return cos_l, sin_a, sin_b, cos.T, sin.T


def _prep_weights(norm_mix_g, w_in, q_norm_g, w_uq, kv_norm_g, w_ukv, sgu_norm_g, w_s, b_s,
                  w_o, norm_ffn_g, w_ff1, w_ff2, final_norm_g):
    kr_end = Q_LORA_RANK + KV_LORA_RANK + QK_ROPE_DIM
    win = jnp.concatenate(
        [w_in[:, :kr_end], jnp.zeros((D_MODEL, LANES - QK_ROPE_DIM), w_in.dtype), w_in[:, kr_end:]],
        axis=1).astype(BF16)
    wq = w_uq.reshape(Q_LORA_RANK, N_HEADS, QK_HEAD_DIM)
    wq = jnp.pad(wq, ((0, 0), (0, 0), (0, QK_PAD_DIM - QK_HEAD_DIM)))
    wqt = wq.reshape(Q_LORA_RANK, N_HEADS * QK_PAD_DIM).T.astype(BF16)
    wkv = w_ukv.reshape(KV_LORA_RANK, N_HEADS, QK_NOPE_DIM + V_HEAD_DIM)
    wk = wkv[:, :, :QK_NOPE_DIM].reshape(KV_LORA_RANK, N_HEADS * QK_NOPE_DIM).astype(BF16)
    wvt = wkv[:, :, QK_NOPE_DIM:].reshape(KV_LORA_RANK, N_HEADS * V_HEAD_DIM).T.astype(BF16)
    bias = jnp.repeat(b_s.T, SGU_WIDTH // SGU_GROUPS, axis=1).astype(F32)
    row = lambda g: g.reshape(1, -1).astype(F32)
    pre_w = (row(norm_mix_g), win, row(q_norm_g), wqt, row(kv_norm_g), wk, wvt, row(sgu_norm_g),
             w_s.astype(BF16), bias)
    post_w = (w_o.astype(BF16), row(norm_ffn_g), w_ff1.astype(BF16), w_ff2.astype(BF16),
              row(final_norm_g))
    return pre_w, post_w


TOKEN_TILE = 512
MAX_Q_TILES_PER_STEP = 8


def _run_stream(x, pre_w, post_w):
    batch, seq, _ = x.shape
    x2d = x.reshape(batch * seq, D_MODEL)
    tables = _rope_tables(seq)
    qt, k, vt, sga, gob = _pre_call(x2d, seq, TOKEN_TILE, *pre_w, *tables)
    n_q = min(MAX_Q_TILES_PER_STEP, seq // TOKEN_TILE)
    oa = _attn_call(qt, k, vt, batch, seq, n_q)
    y = _post_call(x2d, oa, sga, gob, *post_w, TOKEN_TILE)
    return y.reshape(batch, seq, D_MODEL)


def kernel(x_prompt, x_sample, norm_mix_g, w_in, q_norm_g, w_uq, kv_norm_g, w_ukv, sgu_norm_g,
           w_s, b_s, w_o, norm_ffn_g, w_ff1, w_ff2, final_norm_g):
    pre_w, post_w = _prep_weights(
        norm_mix_g[0], w_in[0], q_norm_g[0], w_uq[0], kv_norm_g[0], w_ukv[0], sgu_norm_g[0],
        w_s[0], b_s[0], w_o[0], norm_ffn_g[0], w_ff1[0], w_ff2[0], final_norm_g)
    return (_run_stream(x_prompt, pre_w, post_w), _run_stream(x_sample, pre_w, post_w))
```

```python
import functools
import math

import jax
import jax.numpy as jnp
from jax import lax
from jax.experimental import pallas as pl
from jax.experimental.pallas import tpu as pltpu

D_MODEL = 1024
N_HEADS = 8
QK_NOPE_DIM = 128
QK_ROPE_DIM = 64
QK_HEAD_DIM = QK_NOPE_DIM + QK_ROPE_DIM
V_HEAD_DIM = 128
V_ROWS = V_HEAD_DIM + 16
Q_LORA_RANK = 384
KV_LORA_RANK = 256
ROPE_BASE = 10000.0
ATTN_SCALE = QK_HEAD_DIM ** -0.5
CHUNK = 128
SGU_GROUPS = 8
SGU_WIDTH = D_MODEL
D_FF = 4 * D_MODEL
NORM_EPS = 1e-6

LANES = 128
QK_PAD_DIM = 2 * LANES
ROPE_HALF = QK_ROPE_DIM // 2
Q_SCALE = ATTN_SCALE * math.log2(math.e)

OFF_CQ = 0
OFF_CKV = OFF_CQ + Q_LORA_RANK
OFF_KR = OFF_CKV + KV_LORA_RANK
OFF_U = OFF_KR + LANES
OFF_VS = OFF_U + SGU_WIDTH
OFF_GA = OFF_VS + SGU_WIDTH
OFF_GB = OFF_GA + D_MODEL
IN_COLS_PAD = OFF_GB + D_MODEL

VMEM_LIMIT_BYTES = 56 * 1024 * 1024

BF16 = jnp.bfloat16
F32 = jnp.float32


def _rms(x, g):
    return x * lax.rsqrt(jnp.mean(x * x, axis=-1, keepdims=True) + NORM_EPS) * g


def _gelu_tanh(x):
    c = math.sqrt(2.0 / math.pi)
    return x * (0.5 * (1.0 + jnp.tanh(c * (x + 0.044715 * (x * x * x)))))


def _sigmoid(x):
    return 1.0 / (1.0 + jnp.exp(-x))


def _rope_block(xr, cos, sin_a, sin_b):
    return (xr * cos
            + pltpu.roll(xr, LANES - ROPE_HALF, 1) * sin_a
            + pltpu.roll(xr, ROPE_HALF, 1) * sin_b)


def _dot(a, b):
    return jnp.dot(a, b, preferred_element_type=F32)


def _pre_kernel(x_ref, gmix_ref, win_ref, gq_ref, wqt_ref, gkv_ref, wk_ref, wvt_ref,
                gsgu_ref, ws_ref, bias_ref, cos_ref, sina_ref, sinb_ref, cost_ref, sint_ref,
                qt_ref, k_ref, vt_ref, sga_ref, gob_ref):
    tm = x_ref.shape[0]
    hn = _rms(x_ref[...], gmix_ref[...]).astype(BF16)

    cq = _dot(hn, win_ref[:, OFF_CQ:OFF_CQ + Q_LORA_RANK])
    cqn_t = jnp.transpose(_rms(cq, gq_ref[...])).astype(BF16)
    qa_t = _dot(wqt_ref[...], cqn_t)
    cos_t = cost_ref[...]
    sin_t = sint_ref[...]
    for h in range(N_HEADS):
        base = h * QK_PAD_DIM
        r1 = base + QK_NOPE_DIM
        r2 = r1 + ROPE_HALF
        r3 = r2 + ROPE_HALF
        x1 = qa_t[r1:r2]
        x2 = qa_t[r2:r3]
        qt_ref[h, 0:QK_NOPE_DIM, :] = (qa_t[base:r1] * Q_SCALE).astype(BF16)
        qt_ref[h, QK_NOPE_DIM:QK_NOPE_DIM + ROPE_HALF, :] = (
            (x1 * cos_t - x2 * sin_t) * Q_SCALE).astype(BF16)
        qt_ref[h, QK_NOPE_DIM + ROPE_HALF:QK_HEAD_DIM, :] = (
            (x2 * cos_t + x1 * sin_t) * Q_SCALE).astype(BF16)
        qt_ref[h, QK_HEAD_DIM:QK_PAD_DIM, :] = jnp.zeros((QK_PAD_DIM - QK_HEAD_DIM, tm), BF16)

    ckv = _dot(hn, win_ref[:, OFF_CKV:OFF_CKV + KV_LORA_RANK])
    ckvn = _rms(ckv, gkv_ref[...])
    kn = _dot(ckvn.astype(BF16), wk_ref[...])
    vv_t = _dot(wvt_ref[...], jnp.transpose(ckvn).astype(BF16))
    kr = _rope_block(_dot(hn, win_ref[:, OFF_KR:OFF_KR + LANES]),
                     cos_ref[...], sina_ref[...], sinb_ref[...]).astype(BF16)
    for h in range(N_HEADS):
        k_ref[h, :, 0:QK_NOPE_DIM] = kn[:, h * QK_NOPE_DIM:(h + 1) * QK_NOPE_DIM].astype(BF16)
        k_ref[h, :, QK_NOPE_DIM:QK_PAD_DIM] = kr
        vt_ref[h, 0:V_HEAD_DIM, :] = vv_t[h * V_HEAD_DIM:(h + 1) * V_HEAD_DIM].astype(BF16)
        vt_ref[h, V_HEAD_DIM:V_ROWS, :] = jnp.ones((V_ROWS - V_HEAD_DIM, tm), BF16)

    u = _gelu_tanh(_dot(hn, win_ref[:, OFF_U:OFF_U + SGU_WIDTH]))
    vs = _gelu_tanh(_dot(hn, win_ref[:, OFF_VS:OFF_VS + SGU_WIDTH]))
    vsn = _rms(vs, gsgu_ref[...]).astype(BF16)
    gb = _sigmoid(_dot(hn, win_ref[:, OFF_GB:OFF_GB + D_MODEL]))
    gd = SGU_WIDTH // SGU_GROUPS
    for c in range(tm // CHUNK):
        rows = slice(c * CHUNK, (c + 1) * CHUNK)
        for g in range(SGU_GROUPS):
            cols = slice(g * gd, (g + 1) * gd)
            mixed = _dot(ws_ref[g], vsn[rows, cols]) + bias_ref[:, cols]
            gob_ref[rows, cols] = (gb[rows, cols] * (u[rows, cols] * mixed)).astype(BF16)

    sga_ref[...] = _sigmoid(_dot(hn, win_ref[:, OFF_GA:OFF_GA + D_MODEL])).astype(BF16)


def _const_spec(shape):
    return pl.BlockSpec(shape, lambda i: (0,) * len(shape), pipeline_mode=pl.Buffered(1))


def _pre_call(x2d, seq, tm, gmix, win, gq, wqt, gkv, wk, wvt, gsgu, ws, bias,
              cos, sin_a, sin_b, cos_t, sin_t):
    t = x2d.shape[0]
    tiles_per_seq = seq // tm
    tab_spec = pl.BlockSpec((tm, LANES), lambda i: (i % tiles_per_seq, 0))
    tab_t_spec = pl.BlockSpec((ROPE_HALF, tm), lambda i: (0, i % tiles_per_seq))
    return pl.pallas_call(
        _pre_kernel,
        grid=(t // tm,),
        in_specs=[
            pl.BlockSpec((tm, D_MODEL), lambda i: (i, 0)),
            _const_spec(gmix.shape), _const_spec(win.shape), _const_spec(gq.shape),
            _const_spec(wqt.shape), _const_spec(gkv.shape), _const_spec(wk.shape),
            _const_spec(wvt.shape), _const_spec(gsgu.shape), _const_spec(ws.shape),
            _const_spec(bias.shape), tab_spec, tab_spec, tab_spec, tab_t_spec, tab_t_spec,
        ],
        out_specs=[
            pl.BlockSpec((N_HEADS, None, QK_PAD_DIM, tm), lambda i: (0, i, 0, 0)),
            pl.BlockSpec((N_HEADS, tm, QK_PAD_DIM), lambda i: (0, i, 0)),
            pl.BlockSpec((N_HEADS, None, V_ROWS, tm), lambda i: (0, i, 0, 0)),
            pl.BlockSpec((tm, D_MODEL), lambda i: (i, 0)),
            pl.BlockSpec((tm, D_MODEL), lambda i: (i, 0)),
        ],
        out_shape=[
            jax.ShapeDtypeStruct((N_HEADS, t // tm, QK_PAD_DIM, tm), BF16),
            jax.ShapeDtypeStruct((N_HEADS, t, QK_PAD_DIM), BF16),
            jax.ShapeDtypeStruct((N_HEADS, t // tm, V_ROWS, tm), BF16),
            jax.ShapeDtypeStruct((t, D_MODEL), BF16),
            jax.ShapeDtypeStruct((t, D_MODEL), BF16),
        ],
        compiler_params=pltpu.CompilerParams(
            dimension_semantics=("arbitrary",), vmem_limit_bytes=VMEM_LIMIT_BYTES),
        name="pre_proj",
    )(x2d, gmix, win, gq, wqt, gkv, wk, wvt, gsgu, ws, bias, cos, sin_a, sin_b, cos_t, sin_t)


PIPE_SLOTS = 4
SCORE_LEAD = 4
KV_BLOCKS_PER_STEP = 2


def _attn_kernel(qt_ref, k_ref, vt_ref, o_ref, *scratch):
    s_refs = scratch[0:PIPE_SLOTS]
    mx_refs = scratch[PIPE_SLOTS:2 * PIPE_SLOTS]
    m_ref, acc_ref = scratch[2 * PIPE_SLOTS:]
    n_q, _, tq = qt_ref.shape[1:]
    n_vblk, _, vblk = vt_ref.shape[1:]
    tk = KV_BLOCKS_PER_STEP * vblk
    n_kv = n_vblk // KV_BLOCKS_PER_STEP
    steps = n_q * n_kv
    sub = tk // 8

    def scores(t, slot):
        qi = t // n_kv
        kj = t % n_kv
        start = pl.multiple_of(kj * tk, tk)
        s_t = _dot(k_ref[0, pl.ds(start, tk), :], qt_ref[0, qi])
        s_refs[slot][...] = s_t
        mx_refs[slot][...] = jnp.max(s_t.reshape(sub, 8, tq), axis=0)

    def update(t, slot):
        qi = t // n_kv
        kj = t % n_kv
        m_old = m_ref[qi]
        m_new = jnp.maximum(m_old, jnp.max(mx_refs[slot][...], axis=0, keepdims=True))
        p_t = jnp.exp2((s_refs[slot][...] - m_new).astype(BF16))
        v_t = jnp.concatenate(
            [vt_ref[0, KV_BLOCKS_PER_STEP * kj + c] for c in range(KV_BLOCKS_PER_STEP)], axis=1)
        acc_ref[qi] = jnp.exp2(m_old - m_new) * acc_ref[qi] + _dot(v_t, p_t)
        m_ref[qi] = m_new

    m_ref[...] = jnp.full(m_ref.shape, -jnp.inf, F32)
    acc_ref[...] = jnp.zeros(acc_ref.shape, F32)
    for t in range(SCORE_LEAD):
        scores(t, t)

    def body(i, carry):
        t0 = PIPE_SLOTS * i
        for u in range(PIPE_SLOTS):
            update(t0 + u, u)
            scores(t0 + SCORE_LEAD + u, (SCORE_LEAD + u) % PIPE_SLOTS)
        return carry

    lax.fori_loop(0, steps // PIPE_SLOTS - 1, body, 0)
    for u in range(PIPE_SLOTS):
        update(steps - PIPE_SLOTS + u, u)
        if u + SCORE_LEAD < PIPE_SLOTS:
            scores(steps - PIPE_SLOTS + SCORE_LEAD + u, SCORE_LEAD + u)

    for qi in range(n_q):
        out_t = acc_ref[qi, 0:V_HEAD_DIM, :] / acc_ref[qi, V_HEAD_DIM:V_HEAD_DIM + 1, :]
        o_ref[qi * tq:(qi + 1) * tq, :] = jnp.transpose(out_t).astype(o_ref.dtype)


def _attn_call(qt, k, vt, batch, seq, n_q):
    _, _, _, tq = qt.shape
    _, _, _, vblk = vt.shape
    tk = KV_BLOCKS_PER_STEP * vblk
    t = batch * seq
    assert seq % (n_q * tq) == 0 and seq % tk == 0 and (n_q * (seq // tk)) % PIPE_SLOTS == 0
    groups = seq // (n_q * tq)
    n_vblk = seq // vblk
    return pl.pallas_call(
        _attn_kernel,
        grid=(batch, N_HEADS, groups),
        in_specs=[
            pl.BlockSpec((1, n_q, QK_PAD_DIM, tq), lambda b, h, i: (h, b * groups + i, 0, 0)),
            pl.BlockSpec((1, seq, QK_PAD_DIM), lambda b, h, i: (h, b, 0)),
            pl.BlockSpec((1, n_vblk, V_ROWS, vblk), lambda b, h, i: (h, b, 0, 0)),
        ],
        out_specs=pl.BlockSpec((n_q * tq, V_HEAD_DIM), lambda b, h, i: (b * groups + i, h)),
        out_shape=jax.ShapeDtypeStruct((t, N_HEADS * V_HEAD_DIM), BF16),
        scratch_shapes=(
            [pltpu.VMEM((tk, tq), F32)] * PIPE_SLOTS
            + [pltpu.VMEM((8, tq), F32)] * PIPE_SLOTS
            + [pltpu.VMEM((n_q, 1, tq), F32),
               pltpu.VMEM((n_q, V_ROWS, tq), F32)]
        ),
        compiler_params=pltpu.CompilerParams(
            dimension_semantics=("arbitrary", "arbitrary", "arbitrary"),
            vmem_limit_bytes=VMEM_LIMIT_BYTES),
        name="mla_attention",
    )(qt, k, vt)


def _post_kernel(x_ref, oa_ref, sga_ref, gob_ref, wo_ref, gffn_ref, w1_ref, w2_ref, gfin_ref,
                 y_ref):
    merged = sga_ref[...].astype(F32) * oa_ref[...].astype(F32) + gob_ref[...].astype(F32)
    x1 = x_ref[...] + _dot(merged.astype(BF16), wo_ref[...])
    hf = _rms(x1, gffn_ref[...]).astype(BF16)
    hid = jnp.square(jnp.maximum(_dot(hf, w1_ref[...]), 0.0)).astype(BF16)
    x2 = x1 + _dot(hid, w2_ref[...])
    y_ref[...] = _rms(x2, gfin_ref[...])


def _post_call(x2d, oa, sga, gob, wo, gffn, w1, w2, gfin, tm):
    t = x2d.shape[0]
    row_spec = pl.BlockSpec((tm, D_MODEL), lambda i: (i, 0))
    return pl.pallas_call(
        _post_kernel,
        grid=(t // tm,),
        in_specs=[row_spec, row_spec, row_spec, row_spec,
                  _const_spec(wo.shape), _const_spec(gffn.shape), _const_spec(w1.shape),
                  _const_spec(w2.shape), _const_spec(gfin.shape)],
        out_specs=row_spec,
        out_shape=jax.ShapeDtypeStruct((t, D_MODEL), F32),
        compiler_params=pltpu.CompilerParams(
            dimension_semantics=("arbitrary",), vmem_limit_bytes=VMEM_LIMIT_BYTES),
        name="post_ffn",
    )(x2d, oa, sga, gob, wo, gffn, w1, w2, gfin)


def _rope_tables(seq):
    pos = jnp.arange(seq, dtype=F32)
    inv = ROPE_BASE ** (-jnp.arange(0, QK_ROPE_DIM, 2, dtype=F32) / QK_ROPE_DIM)
    ang = pos[:, None] * inv[None, :]
    cos, sin = jnp.cos(ang), jnp.sin(ang)
    zero = jnp.zeros_like(cos)
    unused = jnp.zeros((seq, LANES - QK_ROPE_DIM), F32)
    cos_l = jnp.concatenate([cos, cos, unused], axis=-1)
    sin_a = jnp.concatenate([-sin, zero, unused], axis=-1)
    sin_b = jnp.concatenate([zero, sin, unused], axis=-1)
    return cos_l, sin_a, sin_b, cos.T, sin.T


def _prep_weights(norm_mix_g, w_in, q_norm_g, w_uq, kv_norm_g, w_ukv, sgu_norm_g, w_s, b_s,
                  w_o, norm_ffn_g, w_ff1, w_ff2, final_norm_g):
    kr_end = Q_LORA_RANK + KV_LORA_RANK + QK_ROPE_DIM
    win = jnp.concatenate(
        [w_in[:, :kr_end], jnp.zeros((D_MODEL, LANES - QK_ROPE_DIM), w_in.dtype), w_in[:, kr_end:]],
        axis=1).astype(BF16)
    wq = w_uq.reshape(Q_LORA_RANK, N_HEADS, QK_HEAD_DIM)
    wq = jnp.pad(wq, ((0, 0), (0, 0), (0, QK_PAD_DIM - QK_HEAD_DIM)))
    wqt = wq.reshape(Q_LORA_RANK, N_HEADS * QK_PAD_DIM).T.astype(BF16)
    wkv = w_ukv.reshape(KV_LORA_RANK, N_HEADS, QK_NOPE_DIM + V_HEAD_DIM)
    wk = wkv[:, :, :QK_NOPE_DIM].reshape(KV_LORA_RANK, N_HEADS * QK_NOPE_DIM).astype(BF16)
    wvt = wkv[:, :, QK_NOPE_DIM:].reshape(KV_LORA_RANK, N_HEADS * V_HEAD_DIM).T.astype(BF16)
    bias = jnp.repeat(b_s.T, SGU_WIDTH // SGU_GROUPS, axis=1).astype(F32)
    row = lambda g: g.reshape(1, -1).astype(F32)
    pre_w = (row(norm_mix_g), win, row(q_norm_g), wqt, row(kv_norm_g), wk, wvt, row(sgu_norm_g),
             w_s.astype(BF16), bias)
    post_w = (w_o.astype(BF16), row(norm_ffn_g), w_ff1.astype(BF16), w_ff2.astype(BF16),
              row(final_norm_g))
    return pre_w, post_w


TOKEN_TILE = 512
MAX_Q_TILES_PER_STEP = 8


def _run_stream(x, pre_w, post_w):
    batch, seq, _ = x.shape
    x2d = x.reshape(batch * seq, D_MODEL)
    tables = _rope_tables(seq)
    qt, k, vt, sga, gob = _pre_call(x2d, seq, TOKEN_TILE, *pre_w, *tables)
    n_q = min(MAX_Q_TILES_PER_STEP, seq // TOKEN_TILE)
    oa = _attn_call(qt, k, vt, batch, seq, n_q)
    y = _post_call(x2d, oa, sga, gob, *post_w, TOKEN_TILE)
    return y.reshape(batch, seq, D_MODEL)


def kernel(x_prompt, x_sample, norm_mix_g, w_in, q_norm_g, w_uq, kv_norm_g, w_ukv, sgu_norm_g,
           w_s, b_s, w_o, norm_ffn_g, w_ff1, w_ff2, final_norm_g):
    pre_w, post_w = _prep_weights(
        norm_mix_g[0], w_in[0], q_norm_g[0], w_uq[0], kv_norm_g[0], w_ukv[0], sgu_norm_g[0],
        w_s[0], b_s[0], w_o[0], norm_ffn_g[0], w_ff1[0], w_ff2[0], final_norm_g)
    return (_run_stream(x_prompt, pre_w, post_w), _run_stream(x_sample, pre_w, post_w))
```

```python
import functools
import math

import jax
import jax.numpy as jnp
from jax import lax
from jax.experimental import pallas as pl
from jax.experimental.pallas import tpu as pltpu

D_MODEL = 1024
N_HEADS = 8
QK_NOPE_DIM = 128
QK_ROPE_DIM = 64
QK_HEAD_DIM = QK_NOPE_DIM + QK_ROPE_DIM
V_HEAD_DIM = 128
V_ROWS = V_HEAD_DIM + 16
Q_LORA_RANK = 384
KV_LORA_RANK = 256
ROPE_BASE = 10000.0
ATTN_SCALE = QK_HEAD_DIM ** -0.5
CHUNK = 128
SGU_GROUPS = 8
SGU_WIDTH = D_MODEL
D_FF = 4 * D_MODEL
NORM_EPS = 1e-6

LANES = 128
QK_PAD_DIM = 2 * LANES
ROPE_HALF = QK_ROPE_DIM // 2
Q_SCALE = ATTN_SCALE * math.log2(math.e)

OFF_CQ = 0
OFF_CKV = OFF_CQ + Q_LORA_RANK
OFF_KR = OFF_CKV + KV_LORA_RANK
OFF_U = OFF_KR + LANES
OFF_VS = OFF_U + SGU_WIDTH
OFF_GA = OFF_VS + SGU_WIDTH
OFF_GB = OFF_GA + D_MODEL
IN_COLS_PAD = OFF_GB + D_MODEL

VMEM_LIMIT_BYTES = 56 * 1024 * 1024

BF16 = jnp.bfloat16
F32 = jnp.float32


def _rms(x, g):
    return x * lax.rsqrt(jnp.mean(x * x, axis=-1, keepdims=True) + NORM_EPS) * g


def _gelu_tanh(x):
    c = math.sqrt(2.0 / math.pi)
    return x * (0.5 * (1.0 + jnp.tanh(c * (x + 0.044715 * (x * x * x)))))


def _sigmoid(x):
    return 1.0 / (1.0 + jnp.exp(-x))


def _rope_block(xr, cos, sin_a, sin_b):
    return (xr * cos
            + pltpu.roll(xr, LANES - ROPE_HALF, 1) * sin_a
            + pltpu.roll(xr, ROPE_HALF, 1) * sin_b)


def _dot(a, b):
    return jnp.dot(a, b, preferred_element_type=F32)


def _pre_kernel(x_ref, gmix_ref, win_ref, gq_ref, wqt_ref, gkv_ref, wk_ref, wvt_ref,
                gsgu_ref, ws_ref, bias_ref, cos_ref, sina_ref, sinb_ref, cost_ref, sint_ref,
                qt_ref, k_ref, vt_ref, sga_ref, gob_ref):
    tm = x_ref.shape[0]
    hn = _rms(x_ref[...], gmix_ref[...]).astype(BF16)

    cq = _dot(hn, win_ref[:, OFF_CQ:OFF_CQ + Q_LORA_RANK])
    cqn_t = jnp.transpose(_rms(cq, gq_ref[...])).astype(BF16)
    qa_t = _dot(wqt_ref[...], cqn_t)
    cos_t = cost_ref[...]
    sin_t = sint_ref[...]
    for h in range(N_HEADS):
        base = h * QK_HEAD_DIM
        r1 = base + QK_NOPE_DIM
        r2 = r1 + ROPE_HALF
        r3 = r2 + ROPE_HALF
        x1 = qa_t[r1:r2]
        x2 = qa_t[r2:r3]
        qt_ref[h, 0:QK_NOPE_DIM, :] = (qa_t[base:r1] * Q_SCALE).astype(BF16)
        qt_ref[h, QK_NOPE_DIM:QK_NOPE_DIM + ROPE_HALF, :] = (
            (x1 * cos_t - x2 * sin_t) * Q_SCALE).astype(BF16)
        qt_ref[h, QK_NOPE_DIM + ROPE_HALF:QK_HEAD_DIM, :] = (
            (x2 * cos_t + x1 * sin_t) * Q_SCALE).astype(BF16)
        qt_ref[h, QK_HEAD_DIM:QK_PAD_DIM, :] = jnp.zeros((QK_PAD_DIM - QK_HEAD_DIM, tm), BF16)

    ckv = _dot(hn, win_ref[:, OFF_CKV:OFF_CKV + KV_LORA_RANK])
    ckvn = _rms(ckv, gkv_ref[...])
    kn = _dot(ckvn.astype(BF16), wk_ref[...])
    vv_t = _dot(wvt_ref[...], jnp.transpose(ckvn).astype(BF16))
    kr = _rope_block(_dot(hn, win_ref[:, OFF_KR:OFF_KR + LANES]),
                     cos_ref[...], sina_ref[...], sinb_ref[...]).astype(BF16)
    for h in range(N_HEADS):
        k_ref[h, :, 0:QK_NOPE_DIM] = kn[:, h * QK_NOPE_DIM:(h + 1) * QK_NOPE_DIM].astype(BF16)
        k_ref[h, :, QK_NOPE_DIM:QK_PAD_DIM] = kr
        vt_ref[h, 0:V_HEAD_DIM, :] = vv_t[h * V_HEAD_DIM:(h + 1) * V_HEAD_DIM].astype(BF16)
        vt_ref[h, V_HEAD_DIM:V_ROWS, :] = jnp.ones((V_ROWS - V_HEAD_DIM, tm), BF16)

    u = _gelu_tanh(_dot(hn, win_ref[:, OFF_U:OFF_U + SGU_WIDTH]))
    vs = _gelu_tanh(_dot(hn, win_ref[:, OFF_VS:OFF_VS + SGU_WIDTH]))
    vsn = _rms(vs, gsgu_ref[...]).astype(BF16)
    gb = _sigmoid(_dot(hn, win_ref[:, OFF_GB:OFF_GB + D_MODEL]))
    gd = SGU_WIDTH // SGU_GROUPS
    n_chunks = tm // CHUNK
    for g in range(SGU_GROUPS):
        cols = slice(g * gd, (g + 1) * gd)
        v_g = jnp.concatenate(
            [vsn[c * CHUNK:(c + 1) * CHUNK, cols] for c in range(n_chunks)], axis=1)
        mixed = _dot(ws_ref[g], v_g)
        for c in range(n_chunks):
            rows = slice(c * CHUNK, (c + 1) * CHUNK)
            mixed_c = mixed[:, c * gd:(c + 1) * gd] + bias_ref[:, cols]
            gob_ref[rows, cols] = (gb[rows, cols] * (u[rows, cols] * mixed_c)).astype(BF16)

    sga_ref[...] = _sigmoid(_dot(hn, win_ref[:, OFF_GA:OFF_GA + D_MODEL])).astype(BF16)


def _const_spec(shape):
    return pl.BlockSpec(shape, lambda i: (0,) * len(shape), pipeline_mode=pl.Buffered(1))


def _pre_call(x2d, seq, tm, gmix, win, gq, wqt, gkv, wk, wvt, gsgu, ws, bias,
              cos, sin_a, sin_b, cos_t, sin_t):
    t = x2d.shape[0]
    tiles_per_seq = seq // tm
    tab_spec = pl.BlockSpec((tm, LANES), lambda i: (i % tiles_per_seq, 0))
    tab_t_spec = pl.BlockSpec((ROPE_HALF, tm), lambda i: (0, i % tiles_per_seq))
    return pl.pallas_call(
        _pre_kernel,
        grid=(t // tm,),
        in_specs=[
            pl.BlockSpec((tm, D_MODEL), lambda i: (i, 0)),
            _const_spec(gmix.shape), _const_spec(win.shape), _const_spec(gq.shape),
            _const_spec(wqt.shape), _const_spec(gkv.shape), _const_spec(wk.shape),
            _const_spec(wvt.shape), _const_spec(gsgu.shape), _const_spec(ws.shape),
            _const_spec(bias.shape), tab_spec, tab_spec, tab_spec, tab_t_spec, tab_t_spec,
        ],
        out_specs=[
            pl.BlockSpec((N_HEADS, None, QK_PAD_DIM, tm), lambda i: (0, i, 0, 0)),
            pl.BlockSpec((N_HEADS, tm, QK_PAD_DIM), lambda i: (0, i, 0)),
            pl.BlockSpec((N_HEADS, None, V_ROWS, tm), lambda i: (0, i, 0, 0)),
            pl.BlockSpec((tm, D_MODEL), lambda i: (i, 0)),
            pl.BlockSpec((tm, D_MODEL), lambda i: (i, 0)),
        ],
        out_shape=[
            jax.ShapeDtypeStruct((N_HEADS, t // tm, QK_PAD_DIM, tm), BF16),
            jax.ShapeDtypeStruct((N_HEADS, t, QK_PAD_DIM), BF16),
            jax.ShapeDtypeStruct((N_HEADS, t // tm, V_ROWS, tm), BF16),
            jax.ShapeDtypeStruct((t, D_MODEL), BF16),
            jax.ShapeDtypeStruct((t, D_MODEL), BF16),
        ],
        compiler_params=pltpu.CompilerParams(
            dimension_semantics=("arbitrary",), vmem_limit_bytes=VMEM_LIMIT_BYTES),
        name="pre_proj",
    )(x2d, gmix, win, gq, wqt, gkv, wk, wvt, gsgu, ws, bias, cos, sin_a, sin_b, cos_t, sin_t)


PIPE_SLOTS = 4
KV_BLOCKS_PER_STEP = 2


def _attn_kernel(qt_ref, k_ref, vt_ref, o_ref, *scratch):
    s_refs = scratch[0:PIPE_SLOTS]
    mx_refs = scratch[PIPE_SLOTS:2 * PIPE_SLOTS]
    m_ref, acc_ref = scratch[2 * PIPE_SLOTS:]
    n_q, _, tq = qt_ref.shape[1:]
    n_vblk, _, vblk = vt_ref.shape[1:]
    tk = KV_BLOCKS_PER_STEP * vblk
    n_kv = n_vblk // KV_BLOCKS_PER_STEP
    steps = n_q * n_kv
    sub = tk // 8

    def scores(t, slot):
        qi = t // n_kv
        kj = t % n_kv
        start = pl.multiple_of(kj * tk, tk)
        s_t = _dot(k_ref[0, pl.ds(start, tk), :], qt_ref[0, qi])
        s_refs[slot][...] = s_t
        mx_refs[slot][...] = jnp.max(s_t.reshape(sub, 8, tq), axis=0)

    def update(t, slot):
        qi = t // n_kv
        kj = t % n_kv
        m_old = m_ref[qi]
        m_new = jnp.maximum(m_old, jnp.max(mx_refs[slot][...], axis=0, keepdims=True))
        p_t = jnp.exp2((s_refs[slot][...] - m_new).astype(BF16))
        v_t = jnp.concatenate(
            [vt_ref[0, KV_BLOCKS_PER_STEP * kj + c] for c in range(KV_BLOCKS_PER_STEP)], axis=1)
        acc_ref[qi] = jnp.exp2(m_old - m_new) * acc_ref[qi] + _dot(v_t, p_t)
        m_ref[qi] = m_new

    m_ref[...] = jnp.full(m_ref.shape, -jnp.inf, F32)
    acc_ref[...] = jnp.zeros(acc_ref.shape, F32)
    for t in range(PIPE_SLOTS):
        scores(t, t)

    def body(i, carry):
        t0 = PIPE_SLOTS * i
        for u in range(PIPE_SLOTS):
            update(t0 + u, u)
            scores(t0 + PIPE_SLOTS + u, u)
        return carry

    lax.fori_loop(0, steps // PIPE_SLOTS - 1, body, 0)
    for u in range(PIPE_SLOTS):
        update(steps - PIPE_SLOTS + u, u)

    for qi in range(n_q):
        out_t = acc_ref[qi, 0:V_HEAD_DIM, :] / acc_ref[qi, V_HEAD_DIM:V_HEAD_DIM + 1, :]
        o_ref[qi * tq:(qi + 1) * tq, :] = jnp.transpose(out_t).astype(o_ref.dtype)


def _attn_call(qt, k, vt, batch, seq, n_q):
    _, _, _, tq = qt.shape
    _, _, _, vblk = vt.shape
    tk = KV_BLOCKS_PER_STEP * vblk
    t = batch * seq
    assert seq % (n_q * tq) == 0 and seq % tk == 0 and (n_q * (seq // tk)) % PIPE_SLOTS == 0
    groups = seq // (n_q * tq)
    n_vblk = seq // vblk
    return pl.pallas_call(
        _attn_kernel,
        grid=(batch, N_HEADS, groups),
        in_specs=[
            pl.BlockSpec((1, n_q, QK_PAD_DIM, tq), lambda b, h, i: (h, b * groups + i, 0, 0)),
            pl.BlockSpec((1, seq, QK_PAD_DIM), lambda b, h, i: (h, b, 0)),
            pl.BlockSpec((1, n_vblk, V_ROWS, vblk), lambda b, h, i: (h, b, 0, 0)),
        ],
        out_specs=pl.BlockSpec((n_q * tq, V_HEAD_DIM), lambda b, h, i: (b * groups + i, h)),
        out_shape=jax.ShapeDtypeStruct((t, N_HEADS * V_HEAD_DIM), BF16),
        scratch_shapes=(
            [pltpu.VMEM((tk, tq), F32)] * PIPE_SLOTS
            + [pltpu.VMEM((8, tq), F32)] * PIPE_SLOTS
            + [pltpu.VMEM((n_q, 1, tq), F32),
               pltpu.VMEM((n_q, V_ROWS, tq), F32)]
        ),
        compiler_params=pltpu.CompilerParams(
            dimension_semantics=("arbitrary", "arbitrary", "arbitrary"),
            vmem_limit_bytes=VMEM_LIMIT_BYTES),
        name="mla_attention",
    )(qt, k, vt)


def _post_kernel(x_ref, oa_ref, sga_ref, gob_ref, wo_ref, gffn_ref, w1_ref, w2_ref, gfin_ref,
                 y_ref):
    merged = sga_ref[...].astype(F32) * oa_ref[...].astype(F32) + gob_ref[...].astype(F32)
    x1 = x_ref[...] + _dot(merged.astype(BF16), wo_ref[...])
    hf = _rms(x1, gffn_ref[...]).astype(BF16)
    hid = jnp.square(jnp.maximum(_dot(hf, w1_ref[...]), 0.0)).astype(BF16)
    x2 = x1 + _dot(hid, w2_ref[...])
    y_ref[...] = _rms(x2, gfin_ref[...])


def _post_call(x2d, oa, sga, gob, wo, gffn, w1, w2, gfin, tm):
    t = x2d.shape[0]
    row_spec = pl.BlockSpec((tm, D_MODEL), lambda i: (i, 0))
    return pl.pallas_call(
        _post_kernel,
        grid=(t // tm,),
        in_specs=[row_spec, row_spec, row_spec, row_spec,
                  _const_spec(wo.shape), _const_spec(gffn.shape), _const_spec(w1.shape),
                  _const_spec(w2.shape), _const_spec(gfin.shape)],
        out_specs=row_spec,
        out_shape=jax.ShapeDtypeStruct((t, D_MODEL), F32),
        compiler_params=pltpu.CompilerParams(
            dimension_semantics=("arbitrary",), vmem_limit_bytes=VMEM_LIMIT_BYTES),
        name="post_ffn",
    )(x2d, oa, sga, gob, wo, gffn, w1, w2, gfin)


def _rope_tables(seq):
    pos = jnp.arange(seq, dtype=F32)
    inv = ROPE_BASE ** (-jnp.arange(0, QK_ROPE_DIM, 2, dtype=F32) / QK_ROPE_DIM)
    ang = pos[:, None] * inv[None, :]
    cos, sin = jnp.cos(ang), jnp.sin(ang)
    zero = jnp.zeros_like(cos)
    unused = jnp.zeros((seq, LANES - QK_ROPE_DIM), F32)
    cos_l = jnp.concatenate([cos, cos, unused], axis=-1)
    sin_a = jnp.concatenate([-sin, zero, unused], axis=-1)
    sin_b = jnp.concatenate([zero, sin, unused], axis=-1)
    return cos_l, sin_a, sin_b, cos.T, sin.T


def _prep_weights(norm_mix_g, w_in, q_norm_g, w_uq, kv_norm_g, w_ukv, sgu_norm_g, w_s, b_s,
                  w_o, norm_ffn_g, w_ff1, w_ff2, final_norm_g):
    kr_end = Q_LORA_RANK + KV_LORA_RANK + QK_ROPE_DIM
    win = jnp.concatenate(
        [w_in[:, :kr_end], jnp.zeros((D_MODEL, LANES - QK_ROPE_DIM), w_in.dtype), w_in[:, kr_end:]],
        axis=1).astype(BF16)
    wqt = w_uq.T.astype(BF16)
    wkv = w_ukv.reshape(KV_LORA_RANK, N_HEADS, QK_NOPE_DIM + V_HEAD_DIM)
    wk = wkv[:, :, :QK_NOPE_DIM].reshape(KV_LORA_RANK, N_HEADS * QK_NOPE_DIM).astype(BF16)
    wvt = wkv[:, :, QK_NOPE_DIM:].reshape(KV_LORA_RANK, N_HEADS * V_HEAD_DIM).T.astype(BF16)
    bias = jnp.repeat(b_s.T, SGU_WIDTH // SGU_GROUPS, axis=1).astype(F32)
    row = lambda g: g.reshape(1, -1).astype(F32)
    pre_w = (row(norm_mix_g), win, row(q_norm_g), wqt, row(kv_norm_g), wk, wvt, row(sgu_norm_g),
             w_s.astype(BF16), bias)
    post_w = (w_o.astype(BF16), row(norm_ffn_g), w_ff1.astype(BF16), w_ff2.astype(BF16),
              row(final_norm_g))
    return pre_w, post_w


TOKEN_TILE = 512
MAX_Q_TILES_PER_STEP = 8


def _run_stream(x, pre_w, post_w):
    batch, seq, _ = x.shape
    x2d = x.reshape(batch * seq, D_MODEL)
    tables = _rope_tables(seq)
    qt, k, vt, sga, gob = _pre_call(x2d, seq, TOKEN_TILE, *pre_w, *tables)
    n_q = min(MAX_Q_TILES_PER_STEP, seq // TOKEN_TILE)
    oa = _attn_call(qt, k, vt, batch, seq, n_q)
    y = _post_call(x2d, oa, sga, gob, *post_w, TOKEN_TILE)
    return y.reshape(batch, seq, D_MODEL)


def kernel(x_prompt, x_sample, norm_mix_g, w_in, q_norm_g, w_uq, kv_norm_g, w_ukv, sgu_norm_g,
           w_s, b_s, w_o, norm_ffn_g, w_ff1, w_ff2, final_norm_g):
    pre_w, post_w = _prep_weights(
        norm_mix_g[0], w_in[0], q_norm_g[0], w_uq[0], kv_norm_g[0], w_ukv[0], sgu_norm_g[0],
        w_s[0], b_s[0], w_o[0], norm_ffn_g[0], w_ff1[0], w_ff2[0], final_norm_g)
    return (_run_stream(x_prompt, pre_w, post_w), _run_stream(x_sample, pre_w, post_w))
```

```python
import functools
import math

import jax
import jax.numpy as jnp
from jax import lax
from jax.experimental import pallas as pl
from jax.experimental.pallas import tpu as pltpu

D_MODEL = 1024
N_HEADS = 8
QK_NOPE_DIM = 128
QK_ROPE_DIM = 64
QK_HEAD_DIM = QK_NOPE_DIM + QK_ROPE_DIM
V_HEAD_DIM = 128
V_ROWS = V_HEAD_DIM + 16
Q_LORA_RANK = 384
KV_LORA_RANK = 256
ROPE_BASE = 10000.0
ATTN_SCALE = QK_HEAD_DIM ** -0.5
CHUNK = 128
SGU_GROUPS = 8
SGU_WIDTH = D_MODEL
D_FF = 4 * D_MODEL
NORM_EPS = 1e-6

LANES = 128
QK_PAD_DIM = 2 * LANES
ROPE_HALF = QK_ROPE_DIM // 2
Q_SCALE = ATTN_SCALE * math.log2(math.e)

OFF_CQ = 0
OFF_CKV = OFF_CQ + Q_LORA_RANK
OFF_KR = OFF_CKV + KV_LORA_RANK
OFF_U = OFF_KR + LANES
OFF_VS = OFF_U + SGU_WIDTH
OFF_GA = OFF_VS + SGU_WIDTH
OFF_GB = OFF_GA + D_MODEL
IN_COLS_PAD = OFF_GB + D_MODEL

VMEM_LIMIT_BYTES = 56 * 1024 * 1024

BF16 = jnp.bfloat16
F32 = jnp.float32


def _rms(x, g):
    return x * lax.rsqrt(jnp.mean(x * x, axis=-1, keepdims=True) + NORM_EPS) * g


def _gelu_tanh(x):
    c = math.sqrt(2.0 / math.pi)
    return x * (0.5 + 0.5 * jnp.tanh(x * (c + (c * 0.044715) * (x * x))))


def _sigmoid(x):
    return 0.5 + 0.5 * jnp.tanh(0.5 * x)


def _rope_block(xr, cos, sin_a, sin_b):
    return (xr * cos
            + pltpu.roll(xr, LANES - ROPE_HALF, 1) * sin_a
            + pltpu.roll(xr, ROPE_HALF, 1) * sin_b)


def _dot(a, b):
    return jnp.dot(a, b, preferred_element_type=F32)


def _pre_kernel(x_ref, gmix_ref, win_ref, gq_ref, wqt_ref, gkv_ref, wk_ref, wvt_ref,
                gsgu_ref, ws_ref, bias_ref, cos_ref, sina_ref, sinb_ref, cost_ref, sint_ref,
                qt_ref, k_ref, vt_ref, sga_ref, gob_ref):
    tm = x_ref.shape[0]
    half = tm // 2
    hn_a = _rms(x_ref[0:half, :], gmix_ref[...]).astype(BF16)
    hn_b = _rms(x_ref[half:tm, :], gmix_ref[...]).astype(BF16)

    def proj_halves(off, width):
        w = win_ref[:, off:off + width]
        return jnp.concatenate([_dot(hn_a, w), _dot(hn_b, w)], axis=0)

    cq = proj_halves(OFF_CQ, Q_LORA_RANK)
    ckv = proj_halves(OFF_CKV, KV_LORA_RANK)
    kr_raw = proj_halves(OFF_KR, LANES)
    hn = jnp.concatenate([hn_a, hn_b], axis=0)
    sga_ref[...] = _sigmoid(_dot(hn, win_ref[:, OFF_GA:OFF_GA + D_MODEL])).astype(BF16)

    cqn_t = jnp.transpose(_rms(cq, gq_ref[...])).astype(BF16)
    ckvn = _rms(ckv, gkv_ref[...])
    ckvn_t = jnp.transpose(ckvn).astype(BF16)

    vs = _gelu_tanh(_dot(hn, win_ref[:, OFF_VS:OFF_VS + SGU_WIDTH]))
    vsn = _rms(vs, gsgu_ref[...]).astype(BF16)
    u = _gelu_tanh(_dot(hn, win_ref[:, OFF_U:OFF_U + SGU_WIDTH]))
    gb = _sigmoid(_dot(hn, win_ref[:, OFF_GB:OFF_GB + D_MODEL]))
    gd = SGU_WIDTH // SGU_GROUPS
    n_chunks = tm // CHUNK
    for g in range(SGU_GROUPS):
        cols = slice(g * gd, (g + 1) * gd)
        v_g = jnp.concatenate(
            [vsn[c * CHUNK:(c + 1) * CHUNK, cols] for c in range(n_chunks)], axis=1)
        mixed = _dot(ws_ref[g], v_g)
        for c in range(n_chunks):
            rows = slice(c * CHUNK, (c + 1) * CHUNK)
            mixed_c = mixed[:, c * gd:(c + 1) * gd] + bias_ref[:, cols]
            gob_ref[rows, cols] = (gb[rows, cols] * (u[rows, cols] * mixed_c)).astype(BF16)

    qa_t = _dot(wqt_ref[...], cqn_t)
    cos_t = cost_ref[...]
    sin_t = sint_ref[...]
    for h in range(N_HEADS):
        base = h * QK_HEAD_DIM
        r1 = base + QK_NOPE_DIM
        r2 = r1 + ROPE_HALF
        r3 = r2 + ROPE_HALF
        x1 = qa_t[r1:r2]
        x2 = qa_t[r2:r3]
        qt_ref[h, 0:QK_NOPE_DIM, :] = (qa_t[base:r1] * Q_SCALE).astype(BF16)
        qt_ref[h, QK_NOPE_DIM:QK_NOPE_DIM + ROPE_HALF, :] = (
            (x1 * cos_t - x2 * sin_t) * Q_SCALE).astype(BF16)
        qt_ref[h, QK_NOPE_DIM + ROPE_HALF:QK_HEAD_DIM, :] = (
            (x2 * cos_t + x1 * sin_t) * Q_SCALE).astype(BF16)
        qt_ref[h, QK_HEAD_DIM:QK_PAD_DIM, :] = jnp.zeros((QK_PAD_DIM - QK_HEAD_DIM, tm), BF16)

    kn = _dot(ckvn.astype(BF16), wk_ref[...])
    vv_t = _dot(wvt_ref[...], ckvn_t)
    kr = _rope_block(kr_raw, cos_ref[...], sina_ref[...], sinb_ref[...]).astype(BF16)
    for h in range(N_HEADS):
        k_ref[h, :, 0:QK_NOPE_DIM] = kn[:, h * QK_NOPE_DIM:(h + 1) * QK_NOPE_DIM].astype(BF16)
        k_ref[h, :, QK_NOPE_DIM:QK_PAD_DIM] = kr
        vt_ref[h, 0:V_HEAD_DIM, :] = vv_t[h * V_HEAD_DIM:(h + 1) * V_HEAD_DIM].astype(BF16)
        vt_ref[h, V_HEAD_DIM:V_ROWS, :] = jnp.ones((V_ROWS - V_HEAD_DIM, tm), BF16)


def _const_spec(shape):
    return pl.BlockSpec(shape, lambda i: (0,) * len(shape), pipeline_mode=pl.Buffered(1))


def _pre_call(x2d, seq, tm, gmix, win, gq, wqt, gkv, wk, wvt, gsgu, ws, bias,
              cos, sin_a, sin_b, cos_t, sin_t):
    t = x2d.shape[0]
    tiles_per_seq = seq // tm
    tab_spec = pl.BlockSpec((tm, LANES), lambda i: (i % tiles_per_seq, 0))
    tab_t_spec = pl.BlockSpec((ROPE_HALF, tm), lambda i: (0, i % tiles_per_seq))
    return pl.pallas_call(
        _pre_kernel,
        grid=(t // tm,),
        in_specs=[
            pl.BlockSpec((tm, D_MODEL), lambda i: (i, 0)),
            _const_spec(gmix.shape), _const_spec(win.shape), _const_spec(gq.shape),
            _const_spec(wqt.shape), _const_spec(gkv.shape), _const_spec(wk.shape),
            _const_spec(wvt.shape), _const_spec(gsgu.shape), _const_spec(ws.shape),
            _const_spec(bias.shape), tab_spec, tab_spec, tab_spec, tab_t_spec, tab_t_spec,
        ],
        out_specs=[
            pl.BlockSpec((N_HEADS, None, QK_PAD_DIM, tm), lambda i: (0, i, 0, 0)),
            pl.BlockSpec((N_HEADS, tm, QK_PAD_DIM), lambda i: (0, i, 0)),
            pl.BlockSpec((N_HEADS, None, V_ROWS, tm), lambda i: (0, i, 0, 0)),
            pl.BlockSpec((tm, D_MODEL), lambda i: (i, 0)),
            pl.BlockSpec((tm, D_MODEL), lambda i: (i, 0)),
        ],
        out_shape=[
            jax.ShapeDtypeStruct((N_HEADS, t // tm, QK_PAD_DIM, tm), BF16),
            jax.ShapeDtypeStruct((N_HEADS, t, QK_PAD_DIM), BF16),
            jax.ShapeDtypeStruct((N_HEADS, t // tm, V_ROWS, tm), BF16),
            jax.ShapeDtypeStruct((t, D_MODEL), BF16),
            jax.ShapeDtypeStruct((t, D_MODEL), BF16),
        ],
        compiler_params=pltpu.CompilerParams(
            dimension_semantics=("arbitrary",), vmem_limit_bytes=VMEM_LIMIT_BYTES),
        name="pre_proj",
    )(x2d, gmix, win, gq, wqt, gkv, wk, wvt, gsgu, ws, bias, cos, sin_a, sin_b, cos_t, sin_t)


PIPE_SLOTS = 4
KV_BLOCKS_PER_STEP = 2


def _attn_kernel(qt_ref, k_ref, vt_ref, o_ref, *scratch):
    s_refs = scratch[0:PIPE_SLOTS]
    mx_refs = scratch[PIPE_SLOTS:2 * PIPE_SLOTS]
    m_ref, acc_ref = scratch[2 * PIPE_SLOTS:]
    n_q, _, tq = qt_ref.shape[1:]
    n_vblk, _, vblk = vt_ref.shape[1:]
    tk = KV_BLOCKS_PER_STEP * vblk
    n_kv = n_vblk // KV_BLOCKS_PER_STEP
    steps = n_q * n_kv
    sub = tk // 8

    def scores(t, slot):
        qi = t // n_kv
        kj = t % n_kv
        start = pl.multiple_of(kj * tk, tk)
        s_t = _dot(k_ref[0, pl.ds(start, tk), :], qt_ref[0, qi])
        s_refs[slot][...] = s_t
        mx_refs[slot][...] = jnp.max(s_t.reshape(sub, 8, tq), axis=0)

    def update(t, slot):
        qi = t // n_kv
        kj = t % n_kv
        m_old = m_ref[qi]
        m_new = jnp.maximum(m_old, jnp.max(mx_refs[slot][...], axis=0, keepdims=True))
        p_t = jnp.exp2((s_refs[slot][...] - m_new).astype(BF16))
        v_t = jnp.concatenate(
            [vt_ref[0, KV_BLOCKS_PER_STEP * kj + c] for c in range(KV_BLOCKS_PER_STEP)], axis=1)
        acc_ref[qi] = jnp.exp2(m_old - m_new) * acc_ref[qi] + _dot(v_t, p_t)
        m_ref[qi] = m_new

    m_ref[...] = jnp.full(m_ref.shape, -jnp.inf, F32)
    acc_ref[...] = jnp.zeros(acc_ref.shape, F32)
    for t in range(PIPE_SLOTS):
        scores(t, t)

    def body(i, carry):
        t0 = PIPE_SLOTS * i
        for u in range(PIPE_SLOTS):
            update(t0 + u, u)
            scores(t0 + PIPE_SLOTS + u, u)
        return carry

    lax.fori_loop(0, steps // PIPE_SLOTS - 1, body, 0)
    for u in range(PIPE_SLOTS):
        update(steps - PIPE_SLOTS + u, u)

    for qi in range(n_q):
        out_t = acc_ref[qi, 0:V_HEAD_DIM, :] / acc_ref[qi, V_HEAD_DIM:V_HEAD_DIM + 1, :]
        o_ref[qi * tq:(qi + 1) * tq, :] = jnp.transpose(out_t).astype(o_ref.dtype)


def _attn_call(qt, k, vt, batch, seq, n_q):
    _, _, _, tq = qt.shape
    _, _, _, vblk = vt.shape
    tk = KV_BLOCKS_PER_STEP * vblk
    t = batch * seq
    assert seq % (n_q * tq) == 0 and seq % tk == 0 and (n_q * (seq // tk)) % PIPE_SLOTS == 0
    groups = seq // (n_q * tq)
    n_vblk = seq // vblk
    return pl.pallas_call(
        _attn_kernel,
        grid=(batch, N_HEADS, groups),
        in_specs=[
            pl.BlockSpec((1, n_q, QK_PAD_DIM, tq), lambda b, h, i: (h, b * groups + i, 0, 0)),
            pl.BlockSpec((1, seq, QK_PAD_DIM), lambda b, h, i: (h, b, 0)),
            pl.BlockSpec((1, n_vblk, V_ROWS, vblk), lambda b, h, i: (h, b, 0, 0)),
        ],
        out_specs=pl.BlockSpec((n_q * tq, V_HEAD_DIM), lambda b, h, i: (b * groups + i, h)),
        out_shape=jax.ShapeDtypeStruct((t, N_HEADS * V_HEAD_DIM), BF16),
        scratch_shapes=(
            [pltpu.VMEM((tk, tq), F32)] * PIPE_SLOTS
            + [pltpu.VMEM((8, tq), F32)] * PIPE_SLOTS
            + [pltpu.VMEM((n_q, 1, tq), F32),
               pltpu.VMEM((n_q, V_ROWS, tq), F32)]
        ),
        compiler_params=pltpu.CompilerParams(
            dimension_semantics=("arbitrary", "arbitrary", "arbitrary"),
            vmem_limit_bytes=VMEM_LIMIT_BYTES),
        name="mla_attention",
    )(qt, k, vt)


def _post_kernel(x_ref, oa_ref, sga_ref, gob_ref, wo_ref, gffn_ref, w1_ref, w2_ref, gfin_ref,
                 y_ref):
    merged = sga_ref[...].astype(F32) * oa_ref[...].astype(F32) + gob_ref[...].astype(F32)
    x1 = x_ref[...] + _dot(merged.astype(BF16), wo_ref[...])
    hf = _rms(x1, gffn_ref[...]).astype(BF16)
    hid = jnp.square(jnp.maximum(_dot(hf, w1_ref[...]), 0.0)).astype(BF16)
    x2 = x1 + _dot(hid, w2_ref[...])
    y_ref[...] = _rms(x2, gfin_ref[...])


def _post_call(x2d, oa, sga, gob, wo, gffn, w1, w2, gfin, tm):
    t = x2d.shape[0]
    row_spec = pl.BlockSpec((tm, D_MODEL), lambda i: (i, 0))
    return pl.pallas_call(
        _post_kernel,
        grid=(t // tm,),
        in_specs=[row_spec, row_spec, row_spec, row_spec,
                  _const_spec(wo.shape), _const_spec(gffn.shape), _const_spec(w1.shape),
                  _const_spec(w2.shape), _const_spec(gfin.shape)],
        out_specs=row_spec,
        out_shape=jax.ShapeDtypeStruct((t, D_MODEL), F32),
        compiler_params=pltpu.CompilerParams(
            dimension_semantics=("arbitrary",), vmem_limit_bytes=VMEM_LIMIT_BYTES),
        name="post_ffn",
    )(x2d, oa, sga, gob, wo, gffn, w1, w2, gfin)


def _rope_tables(seq):
    pos = jnp.arange(seq, dtype=F32)
    inv = ROPE_BASE ** (-jnp.arange(0, QK_ROPE_DIM, 2, dtype=F32) / QK_ROPE_DIM)
    ang = pos[:, None] * inv[None, :]
    cos, sin = jnp.cos(ang), jnp.sin(ang)
    zero = jnp.zeros_like(cos)
    unused = jnp.zeros((seq, LANES - QK_ROPE_DIM), F32)
    cos_l = jnp.concatenate([cos, cos, unused], axis=-1)
    sin_a = jnp.concatenate([-sin, zero, unused], axis=-1)
    sin_b = jnp.concatenate([zero, sin, unused], axis=-1)
    return cos_l, sin_a, sin_b, cos.T, sin.T


def _prep_weights(norm_mix_g, w_in, q_norm_g, w_uq, kv_norm_g, w_ukv, sgu_norm_g, w_s, b_s,
                  w_o, norm_ffn_g, w_ff1, w_ff2, final_norm_g):
    kr_end = Q_LORA_RANK + KV_LORA_RANK + QK_ROPE_DIM
    win = jnp.concatenate(
        [w_in[:, :kr_end], jnp.zeros((D_MODEL, LANES - QK_ROPE_DIM), w_in.dtype), w_in[:, kr_end:]],
        axis=1).astype(BF16)
    wqt = w_uq.T.astype(BF16)
    wkv = w_ukv.reshape(KV_LORA_RANK, N_HEADS, QK_NOPE_DIM + V_HEAD_DIM)
    wk = wkv[:, :, :QK_NOPE_DIM].reshape(KV_LORA_RANK, N_HEADS * QK_NOPE_DIM).astype(BF16)
    wvt = wkv[:, :, QK_NOPE_DIM:].reshape(KV_LORA_RANK, N_HEADS * V_HEAD_DIM).T.astype(BF16)
    bias = jnp.repeat(b_s.T, SGU_WIDTH // SGU_GROUPS, axis=1).astype(F32)
    row = lambda g: g.reshape(1, -1).astype(F32)
    pre_w = (row(norm_mix_g), win, row(q_norm_g), wqt, row(kv_norm_g), wk, wvt, row(sgu_norm_g),
             w_s.astype(BF16), bias)
    post_w = (w_o.astype(BF16), row(norm_ffn_g), w_ff1.astype(BF16), w_ff2.astype(BF16),
              row(final_norm_g))
    return pre_w, post_w


TOKEN_TILE = 512
MAX_Q_TILES_PER_STEP = 8


def _run_stream(x, pre_w, post_w):
    batch, seq, _ = x.shape
    x2d = x.reshape(batch * seq, D_MODEL)
    tables = _rope_tables(seq)
    qt, k, vt, sga, gob = _pre_call(x2d, seq, TOKEN_TILE, *pre_w, *tables)
    n_q = min(MAX_Q_TILES_PER_STEP, seq // TOKEN_TILE)
    oa = _attn_call(qt, k, vt, batch, seq, n_q)
    y = _post_call(x2d, oa, sga, gob, *post_w, TOKEN_TILE)
    return y.reshape(batch, seq, D_MODEL)


def kernel(x_prompt, x_sample, norm_mix_g, w_in, q_norm_g, w_uq, kv_norm_g, w_ukv, sgu_norm_g,
           w_s, b_s, w_o, norm_ffn_g, w_ff1, w_ff2, final_norm_g):
    pre_w, post_w = _prep_weights(
        norm_mix_g[0], w_in[0], q_norm_g[0], w_uq[0], kv_norm_g[0], w_ukv[0], sgu_norm_g[0],
        w_s[0], b_s[0], w_o[0], norm_ffn_g[0], w_ff1[0], w_ff2[0], final_norm_g)
    return (_run_stream(x_prompt, pre_w, post_w), _run_stream(x_sample, pre_w, post_w))
```

```python
import functools
import math

import jax
import jax.numpy as jnp
from jax import lax
from jax.experimental import pallas as pl
from jax.experimental.pallas import tpu as pltpu

D_MODEL = 1024
N_HEADS = 8
QK_NOPE_DIM = 128
QK_ROPE_DIM = 64
QK_HEAD_DIM = QK_NOPE_DIM + QK_ROPE_DIM
V_HEAD_DIM = 128
V_ROWS = V_HEAD_DIM + 16
Q_LORA_RANK = 384
KV_LORA_RANK = 256
ROPE_BASE = 10000.0
ATTN_SCALE = QK_HEAD_DIM ** -0.5
CHUNK = 128
SGU_GROUPS = 8
SGU_WIDTH = D_MODEL
D_FF = 4 * D_MODEL
NORM_EPS = 1e-6

LANES = 128
QK_PAD_DIM = 2 * LANES
ROPE_HALF = QK_ROPE_DIM // 2
Q_SCALE = ATTN_SCALE * math.log2(math.e)

OFF_CQ = 0
OFF_CKV = OFF_CQ + Q_LORA_RANK
OFF_KR = OFF_CKV + KV_LORA_RANK
OFF_U = OFF_KR + LANES
OFF_VS = OFF_U + SGU_WIDTH
OFF_GA = OFF_VS + SGU_WIDTH
OFF_GB = OFF_GA + D_MODEL
IN_COLS_PAD = OFF_GB + D_MODEL

VMEM_LIMIT_BYTES = 56 * 1024 * 1024

BF16 = jnp.bfloat16
F32 = jnp.float32


def _rms(x, g):
    return x * lax.rsqrt(jnp.mean(x * x, axis=-1, keepdims=True) + NORM_EPS) * g


def _gelu_tanh(x):
    c = math.sqrt(2.0 / math.pi)
    return x * (0.5 + 0.5 * jnp.tanh(x * (c + (c * 0.044715) * (x * x))))


def _sigmoid(x):
    return 0.5 + 0.5 * jnp.tanh(0.5 * x)


def _rope_block(xr, cos, sin_a, sin_b):
    return (xr * cos
            + pltpu.roll(xr, LANES - ROPE_HALF, 1) * sin_a
            + pltpu.roll(xr, ROPE_HALF, 1) * sin_b)


def _dot(a, b):
    return jnp.dot(a, b, preferred_element_type=F32)


def _pre_kernel(x_ref, gmix_ref, win_ref, gq_ref, wqt_ref, gkv_ref, wk_ref, wvt_ref,
                gsgu_ref, ws_ref, bias_ref, cos_ref, sina_ref, sinb_ref, cost_ref, sint_ref,
                qt_ref, k_ref, vt_ref, sga_ref, gob_ref):
    tm = x_ref.shape[0]
    half = tm // 2
    hn_a = _rms(x_ref[0:half, :], gmix_ref[...]).astype(BF16)
    hn_b = _rms(x_ref[half:tm, :], gmix_ref[...]).astype(BF16)

    def proj_halves(off, width):
        w = win_ref[:, off:off + width]
        return jnp.concatenate([_dot(hn_a, w), _dot(hn_b, w)], axis=0)

    cq = proj_halves(OFF_CQ, Q_LORA_RANK)
    ckv = proj_halves(OFF_CKV, KV_LORA_RANK)
    kr_raw = proj_halves(OFF_KR, LANES)
    hn = jnp.concatenate([hn_a, hn_b], axis=0)
    sga_ref[...] = _sigmoid(_dot(hn, win_ref[:, OFF_GA:OFF_GA + D_MODEL])).astype(BF16)

    cqn_t = jnp.transpose(_rms(cq, gq_ref[...])).astype(BF16)
    ckvn = _rms(ckv, gkv_ref[...])
    ckvn_t = jnp.transpose(ckvn).astype(BF16)

    vs = _gelu_tanh(_dot(hn, win_ref[:, OFF_VS:OFF_VS + SGU_WIDTH]))
    vsn = _rms(vs, gsgu_ref[...]).astype(BF16)
    u = _gelu_tanh(_dot(hn, win_ref[:, OFF_U:OFF_U + SGU_WIDTH]))
    gb = _sigmoid(_dot(hn, win_ref[:, OFF_GB:OFF_GB + D_MODEL]))
    gd = SGU_WIDTH // SGU_GROUPS
    n_chunks = tm // CHUNK
    for g in range(SGU_GROUPS):
        cols = slice(g * gd, (g + 1) * gd)
        v_g = jnp.concatenate(
            [vsn[c * CHUNK:(c + 1) * CHUNK, cols] for c in range(n_chunks)], axis=1)
        mixed = _dot(ws_ref[g], v_g)
        for c in range(n_chunks):
            rows = slice(c * CHUNK, (c + 1) * CHUNK)
            mixed_c = mixed[:, c * gd:(c + 1) * gd] + bias_ref[:, cols]
            gob_ref[rows, cols] = (gb[rows, cols] * (u[rows, cols] * mixed_c)).astype(BF16)

    qa_t = _dot(wqt_ref[...], cqn_t)
    cos_t = cost_ref[...]
    sin_t = sint_ref[...]
    for h in range(N_HEADS):
        base = h * QK_HEAD_DIM
        r1 = base + QK_NOPE_DIM
        r2 = r1 + ROPE_HALF
        r3 = r2 + ROPE_HALF
        x1 = qa_t[r1:r2]
        x2 = qa_t[r2:r3]
        qt_ref[h, 0:QK_NOPE_DIM, :] = (qa_t[base:r1] * Q_SCALE).astype(BF16)
        qt_ref[h, QK_NOPE_DIM:QK_NOPE_DIM + ROPE_HALF, :] = (
            (x1 * cos_t - x2 * sin_t) * Q_SCALE).astype(BF16)
        qt_ref[h, QK_NOPE_DIM + ROPE_HALF:QK_HEAD_DIM, :] = (
            (x2 * cos_t + x1 * sin_t) * Q_SCALE).astype(BF16)
        qt_ref[h, QK_HEAD_DIM:QK_PAD_DIM, :] = jnp.zeros((QK_PAD_DIM - QK_HEAD_DIM, tm), BF16)

    kn = _dot(ckvn.astype(BF16), wk_ref[...])
    vv_t = _dot(wvt_ref[...], ckvn_t)
    kr = _rope_block(kr_raw, cos_ref[...], sina_ref[...], sinb_ref[...]).astype(BF16)
    for h in range(N_HEADS):
        k_ref[h, :, 0:QK_NOPE_DIM] = kn[:, h * QK_NOPE_DIM:(h + 1) * QK_NOPE_DIM].astype(BF16)
        k_ref[h, :, QK_NOPE_DIM:QK_PAD_DIM] = kr
        vt_ref[h, 0:V_HEAD_DIM, :] = vv_t[h * V_HEAD_DIM:(h + 1) * V_HEAD_DIM].astype(BF16)
        vt_ref[h, V_HEAD_DIM:V_ROWS, :] = jnp.ones((V_ROWS - V_HEAD_DIM, tm), BF16)


def _const_spec(shape):
    return pl.BlockSpec(shape, lambda i: (0,) * len(shape), pipeline_mode=pl.Buffered(1))


def _pre_call(x2d, seq, tm, gmix, win, gq, wqt, gkv, wk, wvt, gsgu, ws, bias,
              cos, sin_a, sin_b, cos_t, sin_t):
    t = x2d.shape[0]
    tiles_per_seq = seq // tm
    tab_spec = pl.BlockSpec((tm, LANES), lambda i: (i % tiles_per_seq, 0))
    tab_t_spec = pl.BlockSpec((ROPE_HALF, tm), lambda i: (0, i % tiles_per_seq))
    return pl.pallas_call(
        _pre_kernel,
        grid=(t // tm,),
        in_specs=[
            pl.BlockSpec((tm, D_MODEL), lambda i: (i, 0)),
            _const_spec(gmix.shape), _const_spec(win.shape), _const_spec(gq.shape),
            _const_spec(wqt.shape), _const_spec(gkv.shape), _const_spec(wk.shape),
            _const_spec(wvt.shape), _const_spec(gsgu.shape), _const_spec(ws.shape),
            _const_spec(bias.shape), tab_spec, tab_spec, tab_spec, tab_t_spec, tab_t_spec,
        ],
        out_specs=[
            pl.BlockSpec((N_HEADS, None, QK_PAD_DIM, tm), lambda i: (0, i, 0, 0)),
            pl.BlockSpec((N_HEADS, tm, QK_PAD_DIM), lambda i: (0, i, 0)),
            pl.BlockSpec((N_HEADS, None, V_ROWS, tm), lambda i: (0, i, 0, 0)),
            pl.BlockSpec((tm, D_MODEL), lambda i: (i, 0)),
            pl.BlockSpec((tm, D_MODEL), lambda i: (i, 0)),
        ],
        out_shape=[
            jax.ShapeDtypeStruct((N_HEADS, t // tm, QK_PAD_DIM, tm), BF16),
            jax.ShapeDtypeStruct((N_HEADS, t, QK_PAD_DIM), BF16),
            jax.ShapeDtypeStruct((N_HEADS, t // tm, V_ROWS, tm), BF16),
            jax.ShapeDtypeStruct((t, D_MODEL), BF16),
            jax.ShapeDtypeStruct((t, D_MODEL), BF16),
        ],
        compiler_params=pltpu.CompilerParams(
            dimension_semantics=("arbitrary",), vmem_limit_bytes=VMEM_LIMIT_BYTES),
        name="pre_proj",
    )(x2d, gmix, win, gq, wqt, gkv, wk, wvt, gsgu, ws, bias, cos, sin_a, sin_b, cos_t, sin_t)


PIPE_SLOTS = 4
SCORE_LEAD = 4
UNROLL = 4
KV_BLOCKS_PER_STEP = 2


def _attn_kernel(qt_ref, k_ref, vt_ref, o_ref, *scratch):
    s_refs = scratch[0:PIPE_SLOTS]
    mx_refs = scratch[PIPE_SLOTS:2 * PIPE_SLOTS]
    m_ref, acc_ref = scratch[2 * PIPE_SLOTS:]
    n_heads, n_q, _, tq = qt_ref.shape
    n_vblk, _, vblk = vt_ref.shape[1:]
    tk = KV_BLOCKS_PER_STEP * vblk
    n_kv = n_vblk // KV_BLOCKS_PER_STEP
    steps = n_heads * n_q * n_kv
    sub = tk // 8

    def where(t):
        row = t // n_kv
        return row, row // n_q, row % n_q, t % n_kv

    def scores(t, slot):
        _, hh, qi, kj = where(t)
        start = pl.multiple_of(kj * tk, tk)
        s_t = _dot(k_ref[hh, pl.ds(start, tk), :], qt_ref[hh, qi])
        s_refs[slot][...] = s_t
        mx_refs[slot][...] = jnp.max(s_t.reshape(sub, 8, tq), axis=0)

    def update(t, slot):
        row, hh, _, kj = where(t)
        m_old = m_ref[row]
        m_new = jnp.maximum(m_old, jnp.max(mx_refs[slot][...], axis=0, keepdims=True))
        p_t = jnp.exp2((s_refs[slot][...] - m_new).astype(BF16))
        v_t = jnp.concatenate(
            [vt_ref[hh, KV_BLOCKS_PER_STEP * kj + c] for c in range(KV_BLOCKS_PER_STEP)], axis=1)
        acc_ref[row] = jnp.exp2(m_old - m_new) * acc_ref[row] + _dot(v_t, p_t)
        m_ref[row] = m_new

    m_ref[...] = jnp.full(m_ref.shape, -jnp.inf, F32)
    acc_ref[...] = jnp.zeros(acc_ref.shape, F32)
    for t in range(SCORE_LEAD):
        scores(t, t % PIPE_SLOTS)

    def body(i, carry):
        t0 = UNROLL * i
        for u in range(UNROLL):
            update(t0 + u, u % PIPE_SLOTS)
            scores(t0 + SCORE_LEAD + u, (SCORE_LEAD + u) % PIPE_SLOTS)
        return carry

    n_bodies = (steps - SCORE_LEAD) // UNROLL
    lax.fori_loop(0, n_bodies, body, 0)
    for t in range(n_bodies * UNROLL, steps):
        update(t, t % PIPE_SLOTS)
        if t + SCORE_LEAD < steps:
            scores(t + SCORE_LEAD, (t + SCORE_LEAD) % PIPE_SLOTS)

    for hh in range(n_heads):
        for qi in range(n_q):
            acc = acc_ref[hh * n_q + qi]
            out_t = acc[0:V_HEAD_DIM, :] / acc[V_HEAD_DIM:V_HEAD_DIM + 1, :]
            o_ref[qi * tq:(qi + 1) * tq, hh * V_HEAD_DIM:(hh + 1) * V_HEAD_DIM] = (
                jnp.transpose(out_t).astype(o_ref.dtype))


def _attn_call(qt, k, vt, batch, seq, n_q, heads_per_step):
    _, _, _, tq = qt.shape
    _, _, _, vblk = vt.shape
    tk = KV_BLOCKS_PER_STEP * vblk
    t = batch * seq
    hb = heads_per_step
    assert seq % (n_q * tq) == 0 and seq % tk == 0 and N_HEADS % hb == 0
    assert hb * n_q * (seq // tk) >= SCORE_LEAD
    groups = seq // (n_q * tq)
    n_vblk = seq // vblk
    return pl.pallas_call(
        _attn_kernel,
        grid=(batch, N_HEADS // hb, groups),
        in_specs=[
            pl.BlockSpec((hb, n_q, QK_PAD_DIM, tq), lambda b, h, i: (h, b * groups + i, 0, 0)),
            pl.BlockSpec((hb, seq, QK_PAD_DIM), lambda b, h, i: (h, b, 0)),
            pl.BlockSpec((hb, n_vblk, V_ROWS, vblk), lambda b, h, i: (h, b, 0, 0)),
        ],
        out_specs=pl.BlockSpec((n_q * tq, hb * V_HEAD_DIM), lambda b, h, i: (b * groups + i, h)),
        out_shape=jax.ShapeDtypeStruct((t, N_HEADS * V_HEAD_DIM), BF16),
        scratch_shapes=(
            [pltpu.VMEM((tk, tq), F32)] * PIPE_SLOTS
            + [pltpu.VMEM((8, tq), F32)] * PIPE_SLOTS
            + [pltpu.VMEM((hb * n_q, 1, tq), F32),
               pltpu.VMEM((hb * n_q, V_ROWS, tq), F32)]
        ),
        compiler_params=pltpu.CompilerParams(
            dimension_semantics=("arbitrary", "arbitrary", "arbitrary"),
            vmem_limit_bytes=VMEM_LIMIT_BYTES),
        name="mla_attention",
    )(qt, k, vt)


def _post_kernel(x_ref, oa_ref, sga_ref, gob_ref, wo_ref, gffn_ref, w1_ref, w2_ref, gfin_ref,
                 y_ref):
    merged = sga_ref[...].astype(F32) * oa_ref[...].astype(F32) + gob_ref[...].astype(F32)
    x1 = x_ref[...] + _dot(merged.astype(BF16), wo_ref[...])
    hf = _rms(x1, gffn_ref[...]).astype(BF16)
    hid = jnp.square(jnp.maximum(_dot(hf, w1_ref[...]), 0.0)).astype(BF16)
    x2 = x1 + _dot(hid, w2_ref[...])
    y_ref[...] = _rms(x2, gfin_ref[...])


def _post_call(x2d, oa, sga, gob, wo, gffn, w1, w2, gfin, tm):
    t = x2d.shape[0]
    row_spec = pl.BlockSpec((tm, D_MODEL), lambda i: (i, 0))
    return pl.pallas_call(
        _post_kernel,
        grid=(t // tm,),
        in_specs=[row_spec, row_spec, row_spec, row_spec,
                  _const_spec(wo.shape), _const_spec(gffn.shape), _const_spec(w1.shape),
                  _const_spec(w2.shape), _const_spec(gfin.shape)],
        out_specs=row_spec,
        out_shape=jax.ShapeDtypeStruct((t, D_MODEL), F32),
        compiler_params=pltpu.CompilerParams(
            dimension_semantics=("arbitrary",), vmem_limit_bytes=VMEM_LIMIT_BYTES),
        name="post_ffn",
    )(x2d, oa, sga, gob, wo, gffn, w1, w2, gfin)


def _rope_tables(seq):
    pos = jnp.arange(seq, dtype=F32)
    inv = ROPE_BASE ** (-jnp.arange(0, QK_ROPE_DIM, 2, dtype=F32) / QK_ROPE_DIM)
    ang = pos[:, None] * inv[None, :]
    cos, sin = jnp.cos(ang), jnp.sin(ang)
    zero = jnp.zeros_like(cos)
    unused = jnp.zeros((seq, LANES - QK_ROPE_DIM), F32)
    cos_l = jnp.concatenate([cos, cos, unused], axis=-1)
    sin_a = jnp.concatenate([-sin, zero, unused], axis=-1)
    sin_b = jnp.concatenate([zero, sin, unused], axis=-1)
    return cos_l, sin_a, sin_b, cos.T, sin.T


def _prep_weights(norm_mix_g, w_in, q_norm_g, w_uq, kv_norm_g, w_ukv, sgu_norm_g, w_s, b_s,
                  w_o, norm_ffn_g, w_ff1, w_ff2, final_norm_g):
    kr_end = Q_LORA_RANK + KV_LORA_RANK + QK_ROPE_DIM
    win = jnp.concatenate(
        [w_in[:, :kr_end], jnp.zeros((D_MODEL, LANES - QK_ROPE_DIM), w_in.dtype), w_in[:, kr_end:]],
        axis=1).astype(BF16)
    wqt = w_uq.T.astype(BF16)
    wkv = w_ukv.reshape(KV_LORA_RANK, N_HEADS, QK_NOPE_DIM + V_HEAD_DIM)
    wk = wkv[:, :, :QK_NOPE_DIM].reshape(KV_LORA_RANK, N_HEADS * QK_NOPE_DIM).astype(BF16)
    wvt = wkv[:, :, QK_NOPE_DIM:].reshape(KV_LORA_RANK, N_HEADS * V_HEAD_DIM).T.astype(BF16)
    bias = jnp.repeat(b_s.T, SGU_WIDTH // SGU_GROUPS, axis=1).astype(F32)
    row = lambda g: g.reshape(1, -1).astype(F32)
    pre_w = (row(norm_mix_g), win, row(q_norm_g), wqt, row(kv_norm_g), wk, wvt, row(sgu_norm_g),
             w_s.astype(BF16), bias)
    post_w = (w_o.astype(BF16), row(norm_ffn_g), w_ff1.astype(BF16), w_ff2.astype(BF16),
              row(final_norm_g))
    return pre_w, post_w


TOKEN_TILE = 512
MAX_Q_TILES_PER_STEP = 8
Q_TILES_PER_STEP = 16


def _run_stream(x, pre_w, post_w):
    batch, seq, _ = x.shape
    x2d = x.reshape(batch * seq, D_MODEL)
    tables = _rope_tables(seq)
    qt, k, vt, sga, gob = _pre_call(x2d, seq, TOKEN_TILE, *pre_w, *tables)
    n_q = min(MAX_Q_TILES_PER_STEP, seq // TOKEN_TILE)
    heads_per_step = 1 if n_q * TOKEN_TILE < seq else max(1, min(N_HEADS, Q_TILES_PER_STEP // n_q))
    oa = _attn_call(qt, k, vt, batch, seq, n_q, heads_per_step)
    y = _post_call(x2d, oa, sga, gob, *post_w, TOKEN_TILE)
    return y.reshape(batch, seq, D_MODEL)


def kernel(x_prompt, x_sample, norm_mix_g, w_in, q_norm_g, w_uq, kv_norm_g, w_ukv, sgu_norm_g,
           w_s, b_s, w_o, norm_ffn_g, w_ff1, w_ff2, final_norm_g):
    pre_w, post_w = _prep_weights(
        norm_mix_g[0], w_in[0], q_norm_g[0], w_uq[0], kv_norm_g[0], w_ukv[0], sgu_norm_g[0],
        w_s[0], b_s[0], w_o[0], norm_ffn_g[0], w_ff1[0], w_ff2[0], final_norm_g)
    return (_run_stream(x_prompt, pre_w, post_w), _run_stream(x_sample, pre_w, post_w))
```

```python
import functools
import math

import jax
import jax.numpy as jnp
from jax import lax
from jax.experimental import pallas as pl
from jax.experimental.pallas import tpu as pltpu

D_MODEL = 1024
N_HEADS = 8
QK_NOPE_DIM = 128
QK_ROPE_DIM = 64
QK_HEAD_DIM = QK_NOPE_DIM + QK_ROPE_DIM
V_HEAD_DIM = 128
V_ROWS = V_HEAD_DIM + 16
Q_LORA_RANK = 384
KV_LORA_RANK = 256
ROPE_BASE = 10000.0
ATTN_SCALE = QK_HEAD_DIM ** -0.5
CHUNK = 128
SGU_GROUPS = 8
SGU_WIDTH = D_MODEL
D_FF = 4 * D_MODEL
NORM_EPS = 1e-6

LANES = 128
QK_PAD_DIM = 2 * LANES
ROPE_HALF = QK_ROPE_DIM // 2
Q_SCALE = ATTN_SCALE * math.log2(math.e)

OFF_CQ = 0
OFF_CKV = OFF_CQ + Q_LORA_RANK
OFF_KR = OFF_CKV + KV_LORA_RANK
LATENT_COLS = OFF_KR + LANES
OFF_U = 0
OFF_VS = OFF_U + SGU_WIDTH
OFF_GA = OFF_VS + SGU_WIDTH
OFF_GB = OFF_GA + D_MODEL

VMEM_LIMIT_BYTES = 56 * 1024 * 1024

BF16 = jnp.bfloat16
F32 = jnp.float32


def _rms(x, g):
    return x * lax.rsqrt(jnp.mean(x * x, axis=-1, keepdims=True) + NORM_EPS) * g


def _gelu_tanh(x):
    c = math.sqrt(2.0 / math.pi)
    return x * (0.5 + 0.5 * jnp.tanh(x * (c + (c * 0.044715) * (x * x))))


def _sigmoid(x):
    return 0.5 + 0.5 * jnp.tanh(0.5 * x)


def _rope_block(xr, cos, sin_a, sin_b):
    return (xr * cos
            + pltpu.roll(xr, LANES - ROPE_HALF, 1) * sin_a
            + pltpu.roll(xr, ROPE_HALF, 1) * sin_b)


def _dot(a, b):
    return jnp.dot(a, b, preferred_element_type=F32)


def _pre_kernel(x_ref, gmix_ref, wlat_ref, wwide_ref, gq_ref, wqt_ref, gkv_ref, wk_ref, wvt_ref,
                gsgu_ref, ws_ref, bias_ref, cos_ref, sina_ref, sinb_ref, cost_ref, sint_ref,
                qt_ref, k_ref, vt_ref, sga_ref, gob_ref):
    tm = x_ref.shape[0]
    half = tm // 2
    hn_a = _rms(x_ref[0:half, :], gmix_ref[...]).astype(BF16)
    hn_b = _rms(x_ref[half:tm, :], gmix_ref[...]).astype(BF16)

    def proj_halves(off, width):
        w = wlat_ref[:, off:off + width]
        return jnp.concatenate([_dot(hn_a, w), _dot(hn_b, w)], axis=0)

    cq = proj_halves(OFF_CQ, Q_LORA_RANK)
    ckv = proj_halves(OFF_CKV, KV_LORA_RANK)
    kr_raw = proj_halves(OFF_KR, LANES)
    hn = jnp.concatenate([hn_a, hn_b], axis=0)
    sga_ref[...] = _sigmoid(_dot(hn, wwide_ref[:, OFF_GA:OFF_GA + D_MODEL])).astype(BF16)

    cqn_t = jnp.transpose(_rms(cq, gq_ref[...])).astype(BF16)
    ckvn = _rms(ckv, gkv_ref[...])
    ckvn_t = jnp.transpose(ckvn).astype(BF16)

    vs = _gelu_tanh(_dot(hn, wwide_ref[:, OFF_VS:OFF_VS + SGU_WIDTH]))
    vsn = _rms(vs, gsgu_ref[...]).astype(BF16)
    u = _gelu_tanh(_dot(hn, wwide_ref[:, OFF_U:OFF_U + SGU_WIDTH]))
    gb = _sigmoid(_dot(hn, wwide_ref[:, OFF_GB:OFF_GB + D_MODEL]))
    gd = SGU_WIDTH // SGU_GROUPS
    n_chunks = tm // CHUNK
    for g in range(SGU_GROUPS):
        cols = slice(g * gd, (g + 1) * gd)
        v_g = jnp.concatenate(
            [vsn[c * CHUNK:(c + 1) * CHUNK, cols] for c in range(n_chunks)], axis=1)
        mixed = _dot(ws_ref[g], v_g)
        for c in range(n_chunks):
            rows = slice(c * CHUNK, (c + 1) * CHUNK)
            mixed_c = mixed[:, c * gd:(c + 1) * gd] + bias_ref[:, cols]
            gob_ref[rows, cols] = (gb[rows, cols] * (u[rows, cols] * mixed_c)).astype(BF16)

    qa_t = _dot(wqt_ref[...], cqn_t)
    cos_t = cost_ref[...]
    sin_t = sint_ref[...]
    for h in range(N_HEADS):
        base = h * QK_HEAD_DIM
        r1 = base + QK_NOPE_DIM
        r2 = r1 + ROPE_HALF
        r3 = r2 + ROPE_HALF
        x1 = qa_t[r1:r2]
        x2 = qa_t[r2:r3]
        qt_ref[h, 0:QK_NOPE_DIM, :] = (qa_t[base:r1] * Q_SCALE).astype(BF16)
        qt_ref[h, QK_NOPE_DIM:QK_NOPE_DIM + ROPE_HALF, :] = (
            (x1 * cos_t - x2 * sin_t) * Q_SCALE).astype(BF16)
        qt_ref[h, QK_NOPE_DIM + ROPE_HALF:QK_HEAD_DIM, :] = (
            (x2 * cos_t + x1 * sin_t) * Q_SCALE).astype(BF16)
        qt_ref[h, QK_HEAD_DIM:QK_PAD_DIM, :] = jnp.zeros((QK_PAD_DIM - QK_HEAD_DIM, tm), BF16)

    kn = _dot(ckvn.astype(BF16), wk_ref[...])
    vv_t = _dot(wvt_ref[...], ckvn_t)
    kr = _rope_block(kr_raw, cos_ref[...], sina_ref[...], sinb_ref[...]).astype(BF16)
    for h in range(N_HEADS):
        k_ref[h, :, 0:QK_NOPE_DIM] = kn[:, h * QK_NOPE_DIM:(h + 1) * QK_NOPE_DIM].astype(BF16)
        k_ref[h, :, QK_NOPE_DIM:QK_PAD_DIM] = kr
        vt_ref[h, 0:V_HEAD_DIM, :] = vv_t[h * V_HEAD_DIM:(h + 1) * V_HEAD_DIM].astype(BF16)
        vt_ref[h, V_HEAD_DIM:V_ROWS, :] = jnp.ones((V_ROWS - V_HEAD_DIM, tm), BF16)


def _const_spec(shape):
    return pl.BlockSpec(shape, lambda i: (0,) * len(shape), pipeline_mode=pl.Buffered(1))


def _pre_call(x2d, seq, tm, gmix, wlat, wwide, gq, wqt, gkv, wk, wvt, gsgu, ws, bias,
              cos, sin_a, sin_b, cos_t, sin_t):
    t = x2d.shape[0]
    tiles_per_seq = seq // tm
    tab_spec = pl.BlockSpec((tm, LANES), lambda i: (i % tiles_per_seq, 0))
    tab_t_spec = pl.BlockSpec((ROPE_HALF, tm), lambda i: (0, i % tiles_per_seq))
    return pl.pallas_call(
        _pre_kernel,
        grid=(t // tm,),
        in_specs=[
            pl.BlockSpec((tm, D_MODEL), lambda i: (i, 0)),
            _const_spec(gmix.shape), _const_spec(wlat.shape), _const_spec(wwide.shape),
            _const_spec(gq.shape),
            _const_spec(wqt.shape), _const_spec(gkv.shape), _const_spec(wk.shape),
            _const_spec(wvt.shape), _const_spec(gsgu.shape), _const_spec(ws.shape),
            _const_spec(bias.shape), tab_spec, tab_spec, tab_spec, tab_t_spec, tab_t_spec,
        ],
        out_specs=[
            pl.BlockSpec((N_HEADS, None, QK_PAD_DIM, tm), lambda i: (0, i, 0, 0)),
            pl.BlockSpec((N_HEADS, tm, QK_PAD_DIM), lambda i: (0, i, 0)),
            pl.BlockSpec((N_HEADS, None, V_ROWS, tm), lambda i: (0, i, 0, 0)),
            pl.BlockSpec((tm, D_MODEL), lambda i: (i, 0)),
            pl.BlockSpec((tm, D_MODEL), lambda i: (i, 0)),
        ],
        out_shape=[
            jax.ShapeDtypeStruct((N_HEADS, t // tm, QK_PAD_DIM, tm), BF16),
            jax.ShapeDtypeStruct((N_HEADS, t, QK_PAD_DIM), BF16),
            jax.ShapeDtypeStruct((N_HEADS, t // tm, V_ROWS, tm), BF16),
            jax.ShapeDtypeStruct((t, D_MODEL), BF16),
            jax.ShapeDtypeStruct((t, D_MODEL), BF16),
        ],
        compiler_params=pltpu.CompilerParams(
            dimension_semantics=("arbitrary",), vmem_limit_bytes=VMEM_LIMIT_BYTES),
        name="pre_proj",
    )(x2d, gmix, wlat, wwide, gq, wqt, gkv, wk, wvt, gsgu, ws, bias, cos, sin_a, sin_b, cos_t, sin_t)


PIPE_SLOTS = 4
SCORE_LEAD = 4
UNROLL = 4
KV_BLOCKS_PER_STEP = 2


def _attn_kernel(qt_ref, k_ref, vt_ref, o_ref, *scratch):
    s_refs = scratch[0:PIPE_SLOTS]
    mx_refs = scratch[PIPE_SLOTS:2 * PIPE_SLOTS]
    m_ref, acc_ref = scratch[2 * PIPE_SLOTS:]
    n_heads, n_q, _, tq = qt_ref.shape
    n_vblk, _, vblk = vt_ref.shape[1:]
    tk = KV_BLOCKS_PER_STEP * vblk
    n_kv = n_vblk // KV_BLOCKS_PER_STEP
    steps = n_heads * n_q * n_kv
    sub = tk // 8

    def where(t):
        row = t // n_kv
        return row, row // n_q, row % n_q, t % n_kv

    def scores(t, slot):
        _, hh, qi, kj = where(t)
        start = pl.multiple_of(kj * tk, tk)
        s_t = _dot(k_ref[hh, pl.ds(start, tk), :], qt_ref[hh, qi])
        s_refs[slot][...] = s_t
        mx_refs[slot][...] = jnp.max(s_t.reshape(sub, 8, tq), axis=0)

    def update(t, slot):
        row, hh, _, kj = where(t)
        m_old = m_ref[row]
        m_new = jnp.maximum(m_old, jnp.max(mx_refs[slot][...], axis=0, keepdims=True))
        p_t = jnp.exp2((s_refs[slot][...] - m_new).astype(BF16))
        v_t = jnp.concatenate(
            [vt_ref[hh, KV_BLOCKS_PER_STEP * kj + c] for c in range(KV_BLOCKS_PER_STEP)], axis=1)
        acc_ref[row] = jnp.exp2(m_old - m_new) * acc_ref[row] + _dot(v_t, p_t)
        m_ref[row] = m_new

    m_ref[...] = jnp.full(m_ref.shape, -jnp.inf, F32)
    acc_ref[...] = jnp.zeros(acc_ref.shape, F32)
    for t in range(SCORE_LEAD):
        scores(t, t % PIPE_SLOTS)

    def body(i, carry):
        t0 = UNROLL * i
        for u in range(UNROLL):
            update(t0 + u, u % PIPE_SLOTS)
            scores(t0 + SCORE_LEAD + u, (SCORE_LEAD + u) % PIPE_SLOTS)
        return carry

    n_bodies = (steps - SCORE_LEAD) // UNROLL
    lax.fori_loop(0, n_bodies, body, 0)
    for t in range(n_bodies * UNROLL, steps):
        update(t, t % PIPE_SLOTS)
        if t + SCORE_LEAD < steps:
            scores(t + SCORE_LEAD, (t + SCORE_LEAD) % PIPE_SLOTS)

    for hh in range(n_heads):
        for qi in range(n_q):
            acc = acc_ref[hh * n_q + qi]
            out_t = acc[0:V_HEAD_DIM, :] / acc[V_HEAD_DIM:V_HEAD_DIM + 1, :]
            o_ref[qi * tq:(qi + 1) * tq, hh * V_HEAD_DIM:(hh + 1) * V_HEAD_DIM] = (
                jnp.transpose(out_t).astype(o_ref.dtype))


def _attn_call(qt, k, vt, batch, seq, n_q, heads_per_step):
    _, _, _, tq = qt.shape
    _, _, _, vblk = vt.shape
    tk = KV_BLOCKS_PER_STEP * vblk
    t = batch * seq
    hb = heads_per_step
    assert seq % (n_q * tq) == 0 and seq % tk == 0 and N_HEADS % hb == 0
    assert hb * n_q * (seq // tk) >= SCORE_LEAD
    groups = seq // (n_q * tq)
    n_vblk = seq // vblk
    return pl.pallas_call(
        _attn_kernel,
        grid=(batch, N_HEADS // hb, groups),
        in_specs=[
            pl.BlockSpec((hb, n_q, QK_PAD_DIM, tq), lambda b, h, i: (h, b * groups + i, 0, 0)),
            pl.BlockSpec((hb, seq, QK_PAD_DIM), lambda b, h, i: (h, b, 0)),
            pl.BlockSpec((hb, n_vblk, V_ROWS, vblk), lambda b, h, i: (h, b, 0, 0)),
        ],
        out_specs=pl.BlockSpec((n_q * tq, hb * V_HEAD_DIM), lambda b, h, i: (b * groups + i, h)),
        out_shape=jax.ShapeDtypeStruct((t, N_HEADS * V_HEAD_DIM), BF16),
        scratch_shapes=(
            [pltpu.VMEM((tk, tq), F32)] * PIPE_SLOTS
            + [pltpu.VMEM((8, tq), F32)] * PIPE_SLOTS
            + [pltpu.VMEM((hb * n_q, 1, tq), F32),
               pltpu.VMEM((hb * n_q, V_ROWS, tq), F32)]
        ),
        compiler_params=pltpu.CompilerParams(
            dimension_semantics=("arbitrary", "arbitrary", "arbitrary"),
            vmem_limit_bytes=VMEM_LIMIT_BYTES),
        name="mla_attention",
    )(qt, k, vt)


def _post_kernel(x_ref, oa_ref, sga_ref, gob_ref, wo_ref, gffn_ref, w1_ref, w2_ref, gfin_ref,
                 y_ref):
    merged = sga_ref[...].astype(F32) * oa_ref[...].astype(F32) + gob_ref[...].astype(F32)
    x1 = x_ref[...] + _dot(merged.astype(BF16), wo_ref[...])
    hf = _rms(x1, gffn_ref[...]).astype(BF16)
    hid = jnp.square(jnp.maximum(_dot(hf, w1_ref[...]), 0.0)).astype(BF16)
    x2 = x1 + _dot(hid, w2_ref[...])
    y_ref[...] = _rms(x2, gfin_ref[...])


def _post_call(x2d, oa, sga, gob, wo, gffn, w1, w2, gfin, tm):
    t = x2d.shape[0]
    row_spec = pl.BlockSpec((tm, D_MODEL), lambda i: (i, 0))
    return pl.pallas_call(
        _post_kernel,
        grid=(t // tm,),
        in_specs=[row_spec, row_spec, row_spec, row_spec,
                  _const_spec(wo.shape), _const_spec(gffn.shape), _const_spec(w1.shape),
                  _const_spec(w2.shape), _const_spec(gfin.shape)],
        out_specs=row_spec,
        out_shape=jax.ShapeDtypeStruct((t, D_MODEL), F32),
        compiler_params=pltpu.CompilerParams(
            dimension_semantics=("arbitrary",), vmem_limit_bytes=VMEM_LIMIT_BYTES),
        name="post_ffn",
    )(x2d, oa, sga, gob, wo, gffn, w1, w2, gfin)


def _rope_tables(seq):
    pos = jnp.arange(seq, dtype=F32)
    inv = ROPE_BASE ** (-jnp.arange(0, QK_ROPE_DIM, 2, dtype=F32) / QK_ROPE_DIM)
    ang = pos[:, None] * inv[None, :]
    cos, sin = jnp.cos(ang), jnp.sin(ang)
    zero = jnp.zeros_like(cos)
    unused = jnp.zeros((seq, LANES - QK_ROPE_DIM), F32)
    cos_l = jnp.concatenate([cos, cos, unused], axis=-1)
    sin_a = jnp.concatenate([-sin, zero, unused], axis=-1)
    sin_b = jnp.concatenate([zero, sin, unused], axis=-1)
    return cos_l, sin_a, sin_b, cos.T, sin.T


def _prep_weights(norm_mix_g, w_in, q_norm_g, w_uq, kv_norm_g, w_ukv, sgu_norm_g, w_s, b_s,
                  w_o, norm_ffn_g, w_ff1, w_ff2, final_norm_g):
    kr_end = Q_LORA_RANK + KV_LORA_RANK + QK_ROPE_DIM
    wlat = jnp.pad(w_in[:, :kr_end], ((0, 0), (0, LATENT_COLS - kr_end))).astype(BF16)
    wwide = w_in[:, kr_end:].astype(BF16)
    wqt = w_uq.T.astype(BF16)
    wkv = w_ukv.reshape(KV_LORA_RANK, N_HEADS, QK_NOPE_DIM + V_HEAD_DIM)
    wk = wkv[:, :, :QK_NOPE_DIM].reshape(KV_LORA_RANK, N_HEADS * QK_NOPE_DIM).astype(BF16)
    wvt = wkv[:, :, QK_NOPE_DIM:].reshape(KV_LORA_RANK, N_HEADS * V_HEAD_DIM).T.astype(BF16)
    bias = jnp.repeat(b_s.T, SGU_WIDTH // SGU_GROUPS, axis=1).astype(F32)
    row = lambda g: g.reshape(1, -1).astype(F32)
    pre_w = (row(norm_mix_g), wlat, wwide, row(q_norm_g), wqt, row(kv_norm_g), wk, wvt,
             row(sgu_norm_g), w_s.astype(BF16), bias)
    post_w = (w_o.astype(BF16), row(norm_ffn_g), w_ff1.astype(BF16), w_ff2.astype(BF16),
              row(final_norm_g))
    return pre_w, post_w


TOKEN_TILE = 512
MAX_Q_TILES_PER_STEP = 8
Q_TILES_PER_STEP = 16


def _run_stream(x, pre_w, post_w, tables):
    batch, seq, _ = x.shape
    x2d = x.reshape(batch * seq, D_MODEL)
    qt, k, vt, sga, gob = _pre_call(x2d, seq, TOKEN_TILE, *pre_w, *tables)
    n_q = min(MAX_Q_TILES_PER_STEP, seq // TOKEN_TILE)
    heads_per_step = 1 if n_q * TOKEN_TILE < seq else max(1, min(N_HEADS, Q_TILES_PER_STEP // n_q))
    oa = _attn_call(qt, k, vt, batch, seq, n_q, heads_per_step)
    y = _post_call(x2d, oa, sga, gob, *post_w, TOKEN_TILE)
    return y.reshape(batch, seq, D_MODEL)


def kernel(x_prompt, x_sample, norm_mix_g, w_in, q_norm_g, w_uq, kv_norm_g, w_ukv, sgu_norm_g,
           w_s, b_s, w_o, norm_ffn_g, w_ff1, w_ff2, final_norm_g):
    pre_w, post_w = _prep_weights(
        norm_mix_g[0], w_in[0], q_norm_g[0], w_uq[0], kv_norm_g[0], w_ukv[0], sgu_norm_g[0],
        w_s[0], b_s[0], w_o[0], norm_ffn_g[0], w_ff1[0], w_ff2[0], final_norm_g)
    tables = _rope_tables(max(x_prompt.shape[1], x_sample.shape[1]))
    return (_run_stream(x_prompt, pre_w, post_w, tables),
            _run_stream(x_sample, pre_w, post_w, tables))
```

```python
import functools
import math

import jax
import jax.numpy as jnp
from jax import lax
from jax.experimental import pallas as pl
from jax.experimental.pallas import tpu as pltpu

D_MODEL = 1024
N_HEADS = 8
QK_NOPE_DIM = 128
QK_ROPE_DIM = 64
QK_HEAD_DIM = QK_NOPE_DIM + QK_ROPE_DIM
V_HEAD_DIM = 128
V_ROWS = V_HEAD_DIM + 16
Q_LORA_RANK = 384
KV_LORA_RANK = 256
ROPE_BASE = 10000.0
ATTN_SCALE = QK_HEAD_DIM ** -0.5
CHUNK = 128
SGU_GROUPS = 8
SGU_WIDTH = D_MODEL
D_FF = 4 * D_MODEL
NORM_EPS = 1e-6

LANES = 128
QK_PAD_DIM = 2 * LANES
ROPE_HALF = QK_ROPE_DIM // 2
Q_SCALE = ATTN_SCALE * math.log2(math.e)

OFF_CQ = 0
OFF_CKV = OFF_CQ + Q_LORA_RANK
OFF_KR = OFF_CKV + KV_LORA_RANK
LATENT_COLS = OFF_KR + LANES
OFF_U = 0
OFF_VS = OFF_U + SGU_WIDTH
OFF_GA = OFF_VS + SGU_WIDTH
OFF_GB = OFF_GA + D_MODEL

VMEM_LIMIT_BYTES = 56 * 1024 * 1024

BF16 = jnp.bfloat16
F32 = jnp.float32


def _rms(x, g):
    return x * lax.rsqrt(jnp.mean(x * x, axis=-1, keepdims=True) + NORM_EPS) * g


def _gelu_tanh(x):
    c = math.sqrt(2.0 / math.pi)
    return x * (0.5 + 0.5 * jnp.tanh(x * (c + (c * 0.044715) * (x * x))))


def _sigmoid(x):
    return 0.5 + 0.5 * jnp.tanh(0.5 * x)


def _rope_block(xr, cos, sin_a, sin_b):
    return (xr * cos
            + pltpu.roll(xr, LANES - ROPE_HALF, 1) * sin_a
            + pltpu.roll(xr, ROPE_HALF, 1) * sin_b)


def _dot(a, b):
    return jnp.dot(a, b, preferred_element_type=F32)


def _pre_kernel(x_ref, gmix_ref, wlat_ref, wwide_ref, gq_ref, wqt_ref, gkv_ref, wk_ref, wvt_ref,
                gsgu_ref, ws_ref, bias_ref, cos_ref, sina_ref, sinb_ref, cost_ref, sint_ref,
                qt_ref, k_ref, vt_ref, sga_ref, gob_ref):
    tm = x_ref.shape[0]
    half = tm // 2
    hn_a = _rms(x_ref[0:half, :], gmix_ref[...]).astype(BF16)
    hn_b = _rms(x_ref[half:tm, :], gmix_ref[...]).astype(BF16)

    def proj_halves(off, width):
        w = wlat_ref[:, off:off + width]
        return jnp.concatenate([_dot(hn_a, w), _dot(hn_b, w)], axis=0)

    cq = proj_halves(OFF_CQ, Q_LORA_RANK)
    ckv = proj_halves(OFF_CKV, KV_LORA_RANK)
    kr_raw = proj_halves(OFF_KR, LANES)
    hn = jnp.concatenate([hn_a, hn_b], axis=0)
    sga_ref[...] = _sigmoid(_dot(hn, wwide_ref[:, OFF_GA:OFF_GA + D_MODEL])).astype(BF16)

    cqn_t = jnp.transpose(_rms(cq, gq_ref[...])).astype(BF16)
    ckvn = _rms(ckv, gkv_ref[...])
    ckvn_t = jnp.transpose(ckvn).astype(BF16)

    vs = _gelu_tanh(_dot(hn, wwide_ref[:, OFF_VS:OFF_VS + SGU_WIDTH]))
    vsn = _rms(vs, gsgu_ref[...]).astype(BF16)
    u = _gelu_tanh(_dot(hn, wwide_ref[:, OFF_U:OFF_U + SGU_WIDTH]))
    gb = _sigmoid(_dot(hn, wwide_ref[:, OFF_GB:OFF_GB + D_MODEL]))
    gd = SGU_WIDTH // SGU_GROUPS
    n_chunks = tm // CHUNK
    for g in range(SGU_GROUPS):
        cols = slice(g * gd, (g + 1) * gd)
        v_g = jnp.concatenate(
            [vsn[c * CHUNK:(c + 1) * CHUNK, cols] for c in range(n_chunks)], axis=1)
        mixed = _dot(ws_ref[g], v_g)
        for c in range(n_chunks):
            rows = slice(c * CHUNK, (c + 1) * CHUNK)
            mixed_c = mixed[:, c * gd:(c + 1) * gd] + bias_ref[:, cols]
            gob_ref[rows, cols] = (gb[rows, cols] * (u[rows, cols] * mixed_c)).astype(BF16)

    qa_t = _dot(wqt_ref[...], cqn_t)
    cos_t = cost_ref[...]
    sin_t = sint_ref[...]
    for h in range(N_HEADS):
        base = h * QK_HEAD_DIM
        r1 = base + QK_NOPE_DIM
        r2 = r1 + ROPE_HALF
        r3 = r2 + ROPE_HALF
        x1 = qa_t[r1:r2]
        x2 = qa_t[r2:r3]
        qt_ref[h, 0:QK_NOPE_DIM, :] = (qa_t[base:r1] * Q_SCALE).astype(BF16)
        qt_ref[h, QK_NOPE_DIM:QK_NOPE_DIM + ROPE_HALF, :] = (
            (x1 * cos_t - x2 * sin_t) * Q_SCALE).astype(BF16)
        qt_ref[h, QK_NOPE_DIM + ROPE_HALF:QK_HEAD_DIM, :] = (
            (x2 * cos_t + x1 * sin_t) * Q_SCALE).astype(BF16)
        qt_ref[h, QK_HEAD_DIM:QK_PAD_DIM, :] = jnp.zeros((QK_PAD_DIM - QK_HEAD_DIM, tm), BF16)

    kn = _dot(ckvn.astype(BF16), wk_ref[...])
    vv_t = _dot(wvt_ref[...], ckvn_t)
    kr = _rope_block(kr_raw, cos_ref[...], sina_ref[...], sinb_ref[...]).astype(BF16)
    for h in range(N_HEADS):
        k_ref[h, :, 0:QK_NOPE_DIM] = kn[:, h * QK_NOPE_DIM:(h + 1) * QK_NOPE_DIM].astype(BF16)
        k_ref[h, :, QK_NOPE_DIM:QK_PAD_DIM] = kr
        vt_ref[h, 0:V_HEAD_DIM, :] = vv_t[h * V_HEAD_DIM:(h + 1) * V_HEAD_DIM].astype(BF16)
        vt_ref[h, V_HEAD_DIM:V_ROWS, :] = jnp.ones((V_ROWS - V_HEAD_DIM, tm), BF16)


def _const_spec(shape):
    return pl.BlockSpec(shape, lambda i: (0,) * len(shape), pipeline_mode=pl.Buffered(1))


def _pre_call(x2d, seq, tm, gmix, wlat, wwide, gq, wqt, gkv, wk, wvt, gsgu, ws, bias,
              cos, sin_a, sin_b, cos_t, sin_t):
    t = x2d.shape[0]
    tiles_per_seq = seq // tm
    tab_spec = pl.BlockSpec((tm, LANES), lambda i: (i % tiles_per_seq, 0))
    tab_t_spec = pl.BlockSpec((ROPE_HALF, tm), lambda i: (0, i % tiles_per_seq))
    return pl.pallas_call(
        _pre_kernel,
        grid=(t // tm,),
        in_specs=[
            pl.BlockSpec((tm, D_MODEL), lambda i: (i, 0)),
            _const_spec(gmix.shape), _const_spec(wlat.shape), _const_spec(wwide.shape),
            _const_spec(gq.shape),
            _const_spec(wqt.shape), _const_spec(gkv.shape), _const_spec(wk.shape),
            _const_spec(wvt.shape), _const_spec(gsgu.shape), _const_spec(ws.shape),
            _const_spec(bias.shape), tab_spec, tab_spec, tab_spec, tab_t_spec, tab_t_spec,
        ],
        out_specs=[
            pl.BlockSpec((N_HEADS, None, QK_PAD_DIM, tm), lambda i: (0, i, 0, 0)),
            pl.BlockSpec((N_HEADS, tm, QK_PAD_DIM), lambda i: (0, i, 0)),
            pl.BlockSpec((N_HEADS, None, V_ROWS, tm), lambda i: (0, i, 0, 0)),
            pl.BlockSpec((tm, D_MODEL), lambda i: (i, 0)),
            pl.BlockSpec((tm, D_MODEL), lambda i: (i, 0)),
        ],
        out_shape=[
            jax.ShapeDtypeStruct((N_HEADS, t // tm, QK_PAD_DIM, tm), BF16),
            jax.ShapeDtypeStruct((N_HEADS, t, QK_PAD_DIM), BF16),
            jax.ShapeDtypeStruct((N_HEADS, t // tm, V_ROWS, tm), BF16),
            jax.ShapeDtypeStruct((t, D_MODEL), BF16),
            jax.ShapeDtypeStruct((t, D_MODEL), BF16),
        ],
        compiler_params=pltpu.CompilerParams(
            dimension_semantics=("arbitrary",), vmem_limit_bytes=VMEM_LIMIT_BYTES),
        name="pre_proj",
    )(x2d, gmix, wlat, wwide, gq, wqt, gkv, wk, wvt, gsgu, ws, bias, cos, sin_a, sin_b, cos_t, sin_t)


PIPE_SLOTS = 4
SCORE_LEAD = 4
UNROLL = 4
KV_BLOCKS_PER_STEP = 2


def _attn_kernel(qt_ref, k_ref, vt_ref, o_ref, *scratch):
    s_refs = scratch[0:PIPE_SLOTS]
    mx_refs = scratch[PIPE_SLOTS:2 * PIPE_SLOTS]
    m_ref, acc_ref = scratch[2 * PIPE_SLOTS:]
    n_heads, n_q, _, tq = qt_ref.shape
    n_vblk, _, vblk = vt_ref.shape[1:]
    tk = KV_BLOCKS_PER_STEP * vblk
    n_kv = n_vblk // KV_BLOCKS_PER_STEP
    steps = n_heads * n_q * n_kv
    sub = tk // 8

    def where(t):
        row = t // n_kv
        return row, row // n_q, row % n_q, t % n_kv

    def scores(t, slot):
        _, hh, qi, kj = where(t)
        start = pl.multiple_of(kj * tk, tk)
        s_t = _dot(k_ref[hh, pl.ds(start, tk), :], qt_ref[hh, qi])
        s_refs[slot][...] = s_t
        mx_refs[slot][...] = jnp.max(s_t.reshape(sub, 8, tq), axis=0)

    assert PIPE_SLOTS == SCORE_LEAD == UNROLL and n_kv % UNROLL == 0 and steps % UNROLL == 0

    def update_group(t0, with_scores):
        row, hh, _, kj0 = where(t0)
        m_old = m_ref[row]
        mx = functools.reduce(jnp.maximum, [mx_refs[u][...] for u in range(UNROLL)])
        m_new = jnp.maximum(m_old, jnp.max(mx, axis=0, keepdims=True))
        acc = jnp.exp2(m_old - m_new) * acc_ref[row]
        for u in range(UNROLL):
            p_t = jnp.exp2((s_refs[u][...] - m_new).astype(BF16))
            v_t = jnp.concatenate(
                [vt_ref[hh, KV_BLOCKS_PER_STEP * (kj0 + u) + c] for c in range(KV_BLOCKS_PER_STEP)],
                axis=1)
            acc = acc + _dot(v_t, p_t)
            if with_scores:
                scores(t0 + SCORE_LEAD + u, u)
        acc_ref[row] = acc
        m_ref[row] = m_new

    m_ref[...] = jnp.full(m_ref.shape, -jnp.inf, F32)
    acc_ref[...] = jnp.zeros(acc_ref.shape, F32)
    for t in range(SCORE_LEAD):
        scores(t, t % PIPE_SLOTS)

    def body(i, carry):
        update_group(UNROLL * i, True)
        return carry

    lax.fori_loop(0, steps // UNROLL - 1, body, 0)
    update_group(steps - UNROLL, False)

    for hh in range(n_heads):
        for qi in range(n_q):
            acc = acc_ref[hh * n_q + qi]
            out_t = acc[0:V_HEAD_DIM, :] / acc[V_HEAD_DIM:V_HEAD_DIM + 1, :]
            o_ref[qi * tq:(qi + 1) * tq, hh * V_HEAD_DIM:(hh + 1) * V_HEAD_DIM] = (
                jnp.transpose(out_t).astype(o_ref.dtype))


def _attn_call(qt, k, vt, batch, seq, n_q, heads_per_step):
    _, _, _, tq = qt.shape
    _, _, _, vblk = vt.shape
    tk = KV_BLOCKS_PER_STEP * vblk
    t = batch * seq
    hb = heads_per_step
    assert seq % (n_q * tq) == 0 and seq % tk == 0 and N_HEADS % hb == 0
    assert hb * n_q * (seq // tk) >= SCORE_LEAD
    groups = seq // (n_q * tq)
    n_vblk = seq // vblk
    return pl.pallas_call(
        _attn_kernel,
        grid=(batch, N_HEADS // hb, groups),
        in_specs=[
            pl.BlockSpec((hb, n_q, QK_PAD_DIM, tq), lambda b, h, i: (h, b * groups + i, 0, 0)),
            pl.BlockSpec((hb, seq, QK_PAD_DIM), lambda b, h, i: (h, b, 0)),
            pl.BlockSpec((hb, n_vblk, V_ROWS, vblk), lambda b, h, i: (h, b, 0, 0)),
        ],
        out_specs=pl.BlockSpec((n_q * tq, hb * V_HEAD_DIM), lambda b, h, i: (b * groups + i, h)),
        out_shape=jax.ShapeDtypeStruct((t, N_HEADS * V_HEAD_DIM), BF16),
        scratch_shapes=(
            [pltpu.VMEM((tk, tq), F32)] * PIPE_SLOTS
            + [pltpu.VMEM((8, tq), F32)] * PIPE_SLOTS
            + [pltpu.VMEM((hb * n_q, 1, tq), F32),
               pltpu.VMEM((hb * n_q, V_ROWS, tq), F32)]
        ),
        compiler_params=pltpu.CompilerParams(
            dimension_semantics=("arbitrary", "arbitrary", "arbitrary"),
            vmem_limit_bytes=VMEM_LIMIT_BYTES),
        name="mla_attention",
    )(qt, k, vt)


def _post_kernel(x_ref, oa_ref, sga_ref, gob_ref, wo_ref, gffn_ref, w1_ref, w2_ref, gfin_ref,
                 y_ref):
    merged = sga_ref[...].astype(F32) * oa_ref[...].astype(F32) + gob_ref[...].astype(F32)
    x1 = x_ref[...] + _dot(merged.astype(BF16), wo_ref[...])
    hf = _rms(x1, gffn_ref[...]).astype(BF16)
    hid = jnp.square(jnp.maximum(_dot(hf, w1_ref[...]), 0.0)).astype(BF16)
    x2 = x1 + _dot(hid, w2_ref[...])
    y_ref[...] = _rms(x2, gfin_ref[...])


def _post_call(x2d, oa, sga, gob, wo, gffn, w1, w2, gfin, tm):
    t = x2d.shape[0]
    row_spec = pl.BlockSpec((tm, D_MODEL), lambda i: (i, 0))
    return pl.pallas_call(
        _post_kernel,
        grid=(t // tm,),
        in_specs=[row_spec, row_spec, row_spec, row_spec,
                  _const_spec(wo.shape), _const_spec(gffn.shape), _const_spec(w1.shape),
                  _const_spec(w2.shape), _const_spec(gfin.shape)],
        out_specs=row_spec,
        out_shape=jax.ShapeDtypeStruct((t, D_MODEL), F32),
        compiler_params=pltpu.CompilerParams(
            dimension_semantics=("arbitrary",), vmem_limit_bytes=VMEM_LIMIT_BYTES),
        name="post_ffn",
    )(x2d, oa, sga, gob, wo, gffn, w1, w2, gfin)


def _rope_tables(seq):
    pos = jnp.arange(seq, dtype=F32)
    inv = ROPE_BASE ** (-jnp.arange(0, QK_ROPE_DIM, 2, dtype=F32) / QK_ROPE_DIM)
    ang = pos[:, None] * inv[None, :]
    cos, sin = jnp.cos(ang), jnp.sin(ang)
    zero = jnp.zeros_like(cos)
    unused = jnp.zeros((seq, LANES - QK_ROPE_DIM), F32)
    cos_l = jnp.concatenate([cos, cos, unused], axis=-1)
    sin_a = jnp.concatenate([-sin, zero, unused], axis=-1)
    sin_b = jnp.concatenate([zero, sin, unused], axis=-1)
    return cos_l, sin_a, sin_b, cos.T, sin.T


def _prep_weights(norm_mix_g, w_in, q_norm_g, w_uq, kv_norm_g, w_ukv, sgu_norm_g, w_s, b_s,
                  w_o, norm_ffn_g, w_ff1, w_ff2, final_norm_g):
    kr_end = Q_LORA_RANK + KV_LORA_RANK + QK_ROPE_DIM
    wlat = jnp.pad(w_in[:, :kr_end], ((0, 0), (0, LATENT_COLS - kr_end))).astype(BF16)
    wwide = w_in[:, kr_end:].astype(BF16)
    wqt = w_uq.T.astype(BF16)
    wkv = w_ukv.reshape(KV_LORA_RANK, N_HEADS, QK_NOPE_DIM + V_HEAD_DIM)
    wk = wkv[:, :, :QK_NOPE_DIM].reshape(KV_LORA_RANK, N_HEADS * QK_NOPE_DIM).astype(BF16)
    wvt = wkv[:, :, QK_NOPE_DIM:].reshape(KV_LORA_RANK, N_HEADS * V_HEAD_DIM).T.astype(BF16)
    bias = jnp.repeat(b_s.T, SGU_WIDTH // SGU_GROUPS, axis=1).astype(F32)
    row = lambda g: g.reshape(1, -1).astype(F32)
    pre_w = (row(norm_mix_g), wlat, wwide, row(q_norm_g), wqt, row(kv_norm_g), wk, wvt,
             row(sgu_norm_g), w_s.astype(BF16), bias)
    post_w = (w_o.astype(BF16), row(norm_ffn_g), w_ff1.astype(BF16), w_ff2.astype(BF16),
              row(final_norm_g))
    return pre_w, post_w


TOKEN_TILE = 512
MAX_Q_TILES_PER_STEP = 8
Q_TILES_PER_STEP = 16


def _run_stream(x, pre_w, post_w, tables):
    batch, seq, _ = x.shape
    x2d = x.reshape(batch * seq, D_MODEL)
    qt, k, vt, sga, gob = _pre_call(x2d, seq, TOKEN_TILE, *pre_w, *tables)
    n_q = min(MAX_Q_TILES_PER_STEP, seq // TOKEN_TILE)
    heads_per_step = 1 if n_q * TOKEN_TILE < seq else max(1, min(N_HEADS, Q_TILES_PER_STEP // n_q))
    oa = _attn_call(qt, k, vt, batch, seq, n_q, heads_per_step)
    y = _post_call(x2d, oa, sga, gob, *post_w, TOKEN_TILE)
    return y.reshape(batch, seq, D_MODEL)


def kernel(x_prompt, x_sample, norm_mix_g, w_in, q_norm_g, w_uq, kv_norm_g, w_ukv, sgu_norm_g,
           w_s, b_s, w_o, norm_ffn_g, w_ff1, w_ff2, final_norm_g):
    pre_w, post_w = _prep_weights(
        norm_mix_g[0], w_in[0], q_norm_g[0], w_uq[0], kv_norm_g[0], w_ukv[0], sgu_norm_g[0],
        w_s[0], b_s[0], w_o[0], norm_ffn_g[0], w_ff1[0], w_ff2[0], final_norm_g)
    tables = _rope_tables(max(x_prompt.shape[1], x_sample.shape[1]))
    return (_run_stream(x_prompt, pre_w, post_w, tables),
            _run_stream(x_sample, pre_w, post_w, tables))
```
